```python
import jax
import jax.numpy as jnp
from jax import lax
import numpy as np

D_MODEL = 4096
BATCH = 2
SEQ = 8192
DEPTH = 2
DEC_BATCH = 4
DEC_SEQ = 4096
PAST_LEN = 128

N_EVEN = (DEPTH + 1) // 2
N_ODD = DEPTH // 2
CHUNK = 128
EPS = 1e-6
CONV_K = 4

A_GROUPS = 4
A_WIDTH = D_MODEL // 2
A_GDIM = A_WIDTH // A_GROUPS

B_HEADS = 8
B_QK = D_MODEL // 16
B_V = 2 * B_QK
B_QK_WIDTH = B_HEADS * B_QK
B_V_WIDTH = B_HEADS * B_V
ROPE_BASE = 10000.0
SPLIT_EVEN = (A_WIDTH, A_WIDTH + B_QK_WIDTH, A_WIDTH + 2 * B_QK_WIDTH,
              A_WIDTH + 2 * B_QK_WIDTH + B_V_WIDTH)
IN_EVEN = A_WIDTH + 2 * B_QK_WIDTH + 2 * B_V_WIDTH
MIX_EVEN = A_WIDTH + B_V_WIDTH

C_WIDTH = D_MODEL // 2
C_BLOCKS = 8
C_BDIM = C_WIDTH // C_BLOCKS
LRU_C = 8.0

D_INNER = D_MODEL
D_HEADDIM = 64
D_HEADS = D_INNER // D_HEADDIM
D_STATE = 128
D_GROUPS = 8
D_HPG = D_HEADS // D_GROUPS
D_BC = D_GROUPS * D_STATE
D_CONV_CH = D_INNER + 2 * D_BC
D_NORM_GROUP = D_INNER // D_GROUPS
SPLIT_ODD = (C_WIDTH, 2 * C_WIDTH, 2 * C_WIDTH + D_INNER,
             2 * C_WIDTH + D_INNER + D_CONV_CH,
             2 * C_WIDTH + D_INNER + D_CONV_CH + D_HEADS)
IN_ODD = 2 * C_WIDTH + D_INNER + D_CONV_CH + 2 * D_HEADS
MIX_ODD = C_WIDTH + D_INNER

D_FF = 2 * D_MODEL
N_EXPERTS = 8
TOP_K = 2
D_FF_EXPERT = D_MODEL // 4

kernel_name = 'hybrid_bidir_encoder_two_batches'


def _rms(x):
    xf = x.astype(jnp.float32)
    return xf * lax.rsqrt(jnp.mean(jnp.square(xf), axis=-1, keepdims=True) + EPS)


def rms_norm(x, g):
    return (_rms(x) * g.astype(jnp.float32)).astype(x.dtype)


def _flip(t):
    return jnp.flip(t, axis=1)


def ada_mod(c, w_mod, b_mod):
    m = jnp.einsum('bd,de->be', jax.nn.silu(c), w_mod) + b_mod
    return jnp.split(m[:, None, :], 6, axis=-1)


def dwconv(u, w, b):
    ch = u.shape[-1]
    y = lax.conv_general_dilated(
        u, w[:, None, :].astype(u.dtype), window_strides=(1,),
        padding=[(CONV_K // 2, CONV_K - 1 - CONV_K // 2)],
        dimension_numbers=('NWC', 'WIO', 'NWC'), feature_group_count=ch)
    return y + b


def rotary(x):
    s, half = x.shape[1], x.shape[-1] // 2
    inv = 1.0 / (ROPE_BASE ** jnp.linspace(0.0, 1.0, half, dtype=jnp.float32))
    ang = jnp.arange(s, dtype=jnp.float32)[:, None] * inv[None, :]
    cos = jnp.cos(ang)[None, :, None, :].astype(x.dtype)
    sin = jnp.sin(ang)[None, :, None, :].astype(x.dtype)
    x1, x2 = x[..., :half], x[..., half:]
    return jnp.concatenate([x1 * cos - x2 * sin, x1 * sin + x2 * cos], axis=-1)


def chunked_decay_scan(q, k, v, log_a, include_diag):
    bsz, s, g, n = q.shape
    h, p = v.shape[-2:]
    nc = s // CHUNK

    def to_chunks(t):
        return jnp.swapaxes(t.reshape((bsz, nc, CHUNK) + t.shape[2:]), 0, 1)

    qc, kc, vc, lc = to_chunks(q), to_chunks(k), to_chunks(v), to_chunks(log_a)
    idx = jnp.arange(CHUNK)
    mask = (idx[:, None] >= idx[None, :]) if include_diag else (idx[:, None] > idx[None, :])

    def step(state, inp):
        qi, ki, vi, li = inp
        cum = jnp.cumsum(li, axis=1)
        cum_t = jnp.moveaxis(cum, 1, -1)
        seg = cum_t[..., :, None] - cum_t[..., None, :]
        decay = jnp.exp(jnp.where(mask, seg, -jnp.inf))
        scores = jnp.einsum('bign,bjgn->bgij', qi, ki)
        w = scores[:, :, None] * decay.astype(scores.dtype)
        y = jnp.einsum('bghij,bjghp->bighp', w, vi)
        y = y + jnp.einsum('bign,bghpn->bighp', qi, state) * jnp.exp(cum)[..., None]
        last = cum[:, -1]
        kv_w = jnp.exp(last[:, None] - cum)
        state = state * jnp.exp(last)[..., None, None] + jnp.einsum(
            'bjgn,bjghp->bghpn', ki, vi * kv_w[..., None].astype(vi.dtype))
        return state, y.astype(v.dtype)

    state0 = jnp.zeros((bsz, g, h, p, n), jnp.float32)
    _, ys = lax.scan(step, state0, (qc, kc, vc, lc))
    return jnp.swapaxes(ys, 0, 1).reshape(bsz, s, g, h, p)


def fourier_mix(u):
    bsz, s, _ = u.shape
    ug = u.reshape(bsz, s, A_GROUPS, A_GDIM).astype(jnp.float32)
    y = jnp.fft.fft2(ug, axes=(1, 3), norm='ortho').real
    return y.reshape(bsz, s, A_WIDTH).astype(u.dtype)


def retention_log_decay(offset):
    return jnp.log1p(-jnp.exp2(-5.0 - offset - jnp.arange(B_HEADS, dtype=jnp.float32)))


def retention_mix(q, k, v, g):
    bsz, s, _ = q.shape
    q = rotary(q.reshape(bsz, s, B_HEADS, B_QK))
    k = rotary(k.reshape(bsz, s, B_HEADS, B_QK)) * (B_QK ** -0.5)
    v = v.reshape(bsz, s, B_HEADS, 1, B_V)
    shape = (bsz, s, B_HEADS, 1)
    la_f = jnp.broadcast_to(retention_log_decay(0.0)[:, None], shape)
    la_b = jnp.broadcast_to(retention_log_decay(0.5)[:, None], shape)
    o = (chunked_decay_scan(q, k, v, la_f, True)
         + _flip(chunked_decay_scan(_flip(q), _flip(k), _flip(v), la_b, False)))
    o = _rms(o.reshape(bsz, s, B_HEADS, B_V)).reshape(bsz, s, B_V_WIDTH)
    return (o * jax.nn.silu(g.astype(jnp.float32))).astype(g.dtype)


def block_diag(x, w, b):
    bsz, s, _ = x.shape
    y = jnp.einsum('bsnd,nde->bsne', x.reshape(bsz, s, C_BLOCKS, C_BDIM), w)
    return y.reshape(bsz, s, C_WIDTH) + b


def _lin_comb(e1, e2):
    a1, b1 = e1
    a2, b2 = e2
    return a1 * a2, a2 * b1 + b2


def rglru_scan(x, wa, ba, wx, bx, lam, reverse):
    r = jax.nn.sigmoid(block_diag(x, wa, ba).astype(jnp.float32))
    i = jax.nn.sigmoid(block_diag(x, wx, bx).astype(jnp.float32))
    log_a = -LRU_C * r * jax.nn.softplus(-lam.astype(jnp.float32))
    a = jnp.exp(log_a)
    b = jnp.sqrt(-jnp.expm1(2.0 * log_a)) * (i * x.astype(jnp.float32))
    _, h = lax.associative_scan(_lin_comb, (a, b), reverse=reverse, axis=1)
    return h


def rglru_mix(gate_in, xin, conv_w, conv_b, wa, ba, wx, bx, lam):
    xc = dwconv(xin, conv_w, conv_b)
    h = (rglru_scan(xc, wa[0], ba[0], wx[0], bx[0], lam[0], False)
         + rglru_scan(xc, wa[1], ba[1], wx[1], bx[1], lam[1], True))
    return (h * jax.nn.gelu(gate_in.astype(jnp.float32))).astype(xin.dtype)


def ssd_mix(z, xbc, dt_f, dt_b, conv_w, conv_b, a_log, dt_bias, d_skip, norm_w):
    bsz, s, _ = z.shape
    xbc = jax.nn.silu(dwconv(xbc, conv_w, conv_b))
    xs, bm, cm = jnp.split(xbc, (D_INNER, D_INNER + D_BC), axis=-1)
    xs = xs.reshape(bsz, s, D_GROUPS, D_HPG, D_HEADDIM)
    bm = bm.reshape(bsz, s, D_GROUPS, D_STATE)
    cm = cm.reshape(bsz, s, D_GROUPS, D_STATE)

    def direction(dt_raw, bias, alog):
        dt = jax.nn.softplus(dt_raw.astype(jnp.float32) + bias.astype(jnp.float32))
        dt = dt.reshape(bsz, s, D_GROUPS, D_HPG)
        log_a = -dt * jnp.exp(alog.astype(jnp.float32)).reshape(D_GROUPS, D_HPG)
        return xs * dt[..., None].astype(xs.dtype), log_a

    v_f, la_f = direction(dt_f, dt_bias[0], a_log[0])
    v_b, la_b = direction(dt_b, dt_bias[1], a_log[1])
    y = (chunked_decay_scan(cm, bm, v_f, la_f, True)
         + _flip(chunked_decay_scan(_flip(cm), _flip(bm), _flip(v_b), _flip(la_b), False))
         + xs * d_skip.reshape(D_GROUPS, D_HPG, 1))
    y = y.reshape(bsz, s, D_INNER) * jax.nn.silu(z)
    y = _rms(y.reshape(bsz, s, D_GROUPS, D_NORM_GROUP)).reshape(bsz, s, D_INNER)
    return (y * norm_w.astype(jnp.float32)).astype(z.dtype)


def swiglu(h, wg, wu, wd):
    a = jnp.einsum('bsd,df->bsf', h, wg)
    u = jnp.einsum('bsd,df->bsf', h, wu)
    return jnp.einsum('bsf,fd->bsd', jax.nn.silu(a) * u, wd)


def moe_swiglu(h, w_router, wg, wu, wd):
    logits = jnp.einsum('bsd,de->bse', h, w_router).astype(jnp.float32)
    top_v, top_i = lax.top_k(logits, TOP_K)
    probs = jax.nn.softmax(top_v, axis=-1)
    gates = jnp.sum(jax.nn.one_hot(top_i, N_EXPERTS, dtype=jnp.float32) * probs[..., None], axis=-2)
    out = jnp.zeros_like(h)
    for e in range(N_EXPERTS):
        out = out + swiglu(h, wg[e], wu[e], wd[e]) * gates[..., e:e + 1].astype(h.dtype)
    return out


def even_layer(x, c, p, j):
    sh1, sc1, gt1, sh2, sc2, gt2 = ada_mod(c, p['w_mod_e'][j], p['b_mod_e'][j])
    h = rms_norm(x, p['g_mix_e'][j]) * (1 + sc1) + sh1
    proj = jnp.einsum('bsd,de->bse', h, p['w_in_e'][j])
    u_a, q, k, v, g = jnp.split(proj, SPLIT_EVEN, axis=-1)
    y = jnp.concatenate([fourier_mix(u_a), retention_mix(q, k, v, g)], axis=-1)
    x = x + gt1 * jnp.einsum('bse,ed->bsd', y, p['w_out_e'][j])
    h = rms_norm(x, p['g_ffn_e'][j]) * (1 + sc2) + sh2
    return x + gt2 * swiglu(h, p['w_gate_e'][j], p['w_up_e'][j], p['w_down_e'][j])


def odd_layer(x, c, p, j):
    sh1, sc1, gt1, sh2, sc2, gt2 = ada_mod(c, p['w_mod_o'][j], p['b_mod_o'][j])
    h = rms_norm(x, p['g_mix_o'][j]) * (1 + sc1) + sh1
    proj = jnp.einsum('bsd,de->bse', h, p['w_in_o'][j])
    g_c, x_c, z, xbc, dt_f, dt_b = jnp.split(proj, SPLIT_ODD, axis=-1)
    y_c = rglru_mix(g_c, x_c, p['conv_w_c'][j], p['conv_b_c'][j], p['lru_wa'][j], p['lru_ba'][j],
                    p['lru_wx'][j], p['lru_bx'][j], p['lru_lam'][j])
    y_d = ssd_mix(z, xbc, dt_f, dt_b, p['conv_w_d'][j], p['conv_b_d'][j], p['ssd_a_log'][j],
                  p['ssd_dt_bias'][j], p['ssd_d'][j], p['ssd_norm_w'][j])
    y = jnp.concatenate([y_c, y_d], axis=-1)
    x = x + gt1 * jnp.einsum('bse,ed->bsd', y, p['w_out_o'][j])
    h = rms_norm(x, p['g_ffn_o'][j]) * (1 + sc2) + sh2
    return x + gt2 * moe_swiglu(h, p['w_router'][j], p['w_gate_x'][j], p['w_up_x'][j], p['w_down_x'][j])


def trunk(x, c, p):
    for layer in range(DEPTH):
        if layer % 2 == 0:
            x = even_layer(x, c, p, layer // 2)
        else:
            x = odd_layer(x, c, p, layer // 2)
    return rms_norm(x, p['g_final'])


def setup_inputs(seed: int = 0) -> dict:
    key = jax.random.key(seed)
    ks = iter(jax.random.split(key, 64))
    f32 = jnp.float32
    D = D_MODEL

    def nrm(shape, scale):
        return jax.random.normal(next(ks), shape, f32) * scale

    def gain(shape):
        return 1.0 + nrm(shape, 0.02)

    def unif(shape, lo, hi):
        return jax.random.uniform(next(ks), shape, f32, lo, hi)

    lam_a = unif((N_ODD, 2, C_WIDTH), 0.9, 0.999) ** (1.0 / LRU_C)
    dt0 = jnp.exp(unif((N_ODD, 2, D_HEADS), float(np.log(1e-3)), float(np.log(1e-1))))
    return {
        'x_prompt': nrm((BATCH, SEQ, D), 1.0),
        'x_sample': nrm((DEC_BATCH, DEC_SEQ, D), 1.0),
        'c_prompt': nrm((BATCH, D), 1.0),
        'c_sample': nrm((DEC_BATCH, D), 1.0),
        'w_mod_e': nrm((N_EVEN, D, 6 * D), 0.5 * D ** -0.5),
        'b_mod_e': nrm((N_EVEN, 6 * D), 0.02),
        'g_mix_e': gain((N_EVEN, D)),
        'g_ffn_e': gain((N_EVEN, D)),
        'w_in_e': nrm((N_EVEN, D, IN_EVEN), D ** -0.5),
        'w_out_e': nrm((N_EVEN, MIX_EVEN, D), MIX_EVEN ** -0.5),
        'w_gate_e': nrm((N_EVEN, D, D_FF), D ** -0.5),
        'w_up_e': nrm((N_EVEN, D, D_FF), D ** -0.5),
        'w_down_e': nrm((N_EVEN, D_FF, D), D_FF ** -0.5),
        'w_mod_o': nrm((N_ODD, D, 6 * D), 0.5 * D ** -0.5),
        'b_mod_o': nrm((N_ODD, 6 * D), 0.02),
        'g_mix_o': gain((N_ODD, D)),
        'g_ffn_o': gain((N_ODD, D)),
        'w_in_o': nrm((N_ODD, D, IN_ODD), D ** -0.5),
        'conv_w_c': nrm((N_ODD, CONV_K, C_WIDTH), 0.5),
        'conv_b_c': nrm((N_ODD, C_WIDTH), 0.02),
        'lru_wa': nrm((N_ODD, 2, C_BLOCKS, C_BDIM, C_BDIM), C_BDIM ** -0.5),
        'lru_ba': nrm((N_ODD, 2, C_WIDTH), 0.02),
        'lru_wx': nrm((N_ODD, 2, C_BLOCKS, C_BDIM, C_BDIM), C_BDIM ** -0.5),
        'lru_bx': nrm((N_ODD, 2, C_WIDTH), 0.02),
        'lru_lam': jnp.log(lam_a) - jnp.log1p(-lam_a),
        'conv_w_d': nrm((N_ODD, CONV_K, D_CONV_CH), 0.5),
        'conv_b_d': nrm((N_ODD, D_CONV_CH), 0.02),
        'ssd_a_log': jnp.log(unif((N_ODD, 2, D_HEADS), 1.0, 16.0)),
        'ssd_dt_bias': dt0 + jnp.log(-jnp.expm1(-dt0)),
        'ssd_d': 1.0 + nrm((N_ODD, D_HEADS), 0.1),
        'ssd_norm_w': gain((N_ODD, D_INNER)),
        'w_out_o': nrm((N_ODD, MIX_ODD, D), MIX_ODD ** -0.5),
        'w_router': nrm((N_ODD, D, N_EXPERTS), D ** -0.5),
        'w_gate_x': nrm((N_ODD, N_EXPERTS, D, D_FF_EXPERT), D ** -0.5),
        'w_up_x': nrm((N_ODD, N_EXPERTS, D, D_FF_EXPERT), D ** -0.5),
        'w_down_x': nrm((N_ODD, N_EXPERTS, D_FF_EXPERT, D), D_FF_EXPERT ** -0.5),
        'g_final': gain((D,)),
    }


def reference(x_prompt, x_sample, c_prompt, c_sample,
              w_mod_e, b_mod_e, g_mix_e, g_ffn_e, w_in_e, w_out_e, w_gate_e, w_up_e, w_down_e,
              w_mod_o, b_mod_o, g_mix_o, g_ffn_o, w_in_o, conv_w_c, conv_b_c,
              lru_wa, lru_ba, lru_wx, lru_bx, lru_lam, conv_w_d, conv_b_d,
              ssd_a_log, ssd_dt_bias, ssd_d, ssd_norm_w, w_out_o,
              w_router, w_gate_x, w_up_x, w_down_x, g_final):
    p = dict(w_mod_e=w_mod_e, b_mod_e=b_mod_e, g_mix_e=g_mix_e, g_ffn_e=g_ffn_e,
             w_in_e=w_in_e, w_out_e=w_out_e, w_gate_e=w_gate_e, w_up_e=w_up_e, w_down_e=w_down_e,
             w_mod_o=w_mod_o, b_mod_o=b_mod_o, g_mix_o=g_mix_o, g_ffn_o=g_ffn_o, w_in_o=w_in_o,
             conv_w_c=conv_w_c, conv_b_c=conv_b_c, lru_wa=lru_wa, lru_ba=lru_ba,
             lru_wx=lru_wx, lru_bx=lru_bx, lru_lam=lru_lam, conv_w_d=conv_w_d, conv_b_d=conv_b_d,
             ssd_a_log=ssd_a_log, ssd_dt_bias=ssd_dt_bias, ssd_d=ssd_d, ssd_norm_w=ssd_norm_w,
             w_out_o=w_out_o, w_router=w_router, w_gate_x=w_gate_x, w_up_x=w_up_x,
             w_down_x=w_down_x, g_final=g_final)
    y_prompt = trunk(x_prompt, c_prompt, p)
    y_sample = trunk(x_sample, c_sample, p)
    return (y_prompt, y_sample)
```

```python
import functools
import math

import jax
import jax.numpy as jnp
from jax import lax
from jax.experimental import pallas as pl
from jax.experimental.pallas import tpu as pltpu

F32 = jnp.float32
BF16 = jnp.bfloat16

EPS = 1e-6
ROPE_BASE = 10000.0
LRU_C = 8.0
CONV_K = 4
A_GROUPS = 4
B_HEADS = 8
C_BLOCKS = 8
D_HEADDIM = 64
D_STATE = 128
D_GROUPS = 8
D_HPG = 8
TOP_K = 2

LANES = 128
BF16_ROWS = 16
SSD_CHUNK = 128
RET_CHUNK = 512
MM_TILE = 1024
MM_TK_MAX = 4096


def _params(sem, vmem_mb):
    return pltpu.CompilerParams(dimension_semantics=sem, vmem_limit_bytes=vmem_mb << 20)


def _dot(a, b):
    return jnp.dot(a, b, preferred_element_type=F32)


def _dot_nt(a, b):
    return lax.dot_general(a, b, (((1,), (1,)), ((), ())), preferred_element_type=F32)


def _sigmoid(x):
    return 1.0 / (1.0 + jnp.exp(-x))


def _silu(x):
    return x / (1.0 + jnp.exp(-x))


def _softplus(x):
    return jnp.maximum(x, 0.0) + jnp.log(1.0 + jnp.exp(-jnp.abs(x)))


def _gelu_tanh(x):
    return 0.5 * x * (1.0 + jnp.tanh(math.sqrt(2.0 / math.pi) * (x + 0.044715 * (x * x * x))))


def _mm_body(*refs, nk, nk1, has_x2, resid):
    refs = list(refs)
    x_ref = refs.pop(0)
    x2_ref = refs.pop(0) if has_x2 else None
    w_ref = refs.pop(0)
    res_ref = refs.pop(0) if resid else None
    gate_ref = refs.pop(0) if resid else None
    o_ref = refs.pop(0)
    acc_ref = refs.pop(0) if nk > 1 else None

    def finish(acc):
        if resid:
            o_ref[...] = res_ref[...] + gate_ref[0] * acc
        else:
            o_ref[...] = acc.astype(o_ref.dtype)

    if nk == 1:
        finish(_dot(x_ref[...], w_ref[...]))
        return

    k = pl.program_id(2)

    @pl.when(k == 0)
    def _():
        acc_ref[...] = _dot(x_ref[...], w_ref[...])

    if has_x2:
        if nk1 > 1:
            @pl.when(jnp.logical_and(k > 0, k < nk1))
            def _():
                acc_ref[...] += _dot(x_ref[...], w_ref[...])

        @pl.when(k >= nk1)
        def _():
            acc_ref[...] += _dot(x2_ref[...], w_ref[...])
    else:
        @pl.when(k > 0)
        def _():
            acc_ref[...] += _dot(x_ref[...], w_ref[...])

    @pl.when(k == nk - 1)
    def _():
        finish(acc_ref[...])


def _matmul(x, w, *, out_dtype, name, x2=None, res=None, gate=None, rows_per_gate=None, tk=None):
    m, k1 = x.shape
    k, n = w.shape
    tm, tn = min(MM_TILE, m), min(MM_TILE, n)
    if tk is None:
        tk = k if k <= MM_TK_MAX else MM_TK_MAX // 2
    if x2 is not None:
        tk = min(tk, math.gcd(k1, x2.shape[1]))
    nk, nk1 = k // tk, k1 // tk
    resid = res is not None
    in_specs = [pl.BlockSpec((tm, tk), lambda i, j, kk: (i, jnp.minimum(kk, nk1 - 1)))]
    args = [x]
    if x2 is not None:
        in_specs.append(pl.BlockSpec((tm, tk), lambda i, j, kk: (i, jnp.maximum(kk - nk1, 0))))
        args.append(x2)
    in_specs.append(pl.BlockSpec((tk, tn), lambda i, j, kk: (kk, j)))
    args.append(w)
    if resid:
        in_specs.append(pl.BlockSpec((tm, tn), lambda i, j, kk: (i, j)))
        in_specs.append(pl.BlockSpec((1, 1, tn), lambda i, j, kk: ((i * tm) // rows_per_gate, 0, j)))
        args += [res, gate]
    scratch = [pltpu.VMEM((tm, tn), F32)] if nk > 1 else []
    return pl.pallas_call(
        functools.partial(_mm_body, nk=nk, nk1=nk1, has_x2=x2 is not None, resid=resid),
        out_shape=jax.ShapeDtypeStruct((m, n), out_dtype),
        grid=(m // tm, n // tn, nk),
        in_specs=in_specs,
        out_specs=pl.BlockSpec((tm, tn), lambda i, j, kk: (i, j)),
        scratch_shapes=scratch,
        compiler_params=_params(("parallel", "parallel", "arbitrary"), 56),
        name=name,
    )(*args)


def _glu_body(*refs, has_gate):
    if has_gate:
        x_ref, wg_ref, wu_ref, gcol_ref, o_ref = refs
    else:
        x_ref, wg_ref, wu_ref, o_ref = refs
    x = x_ref[...]
    a = _dot(x, wg_ref[0])
    u = _dot(x, wu_ref[0])
    h = _silu(a) * u
    if has_gate:
        h = h * gcol_ref[0]
    o_ref[...] = h.astype(o_ref.dtype)


def _glu_up(x, wg, wu, gates_t, *, name):
    m, k = x.shape
    e, _, f = wg.shape
    tm, tn = min(MM_TILE, m), min(MM_TILE // 2, f)
    per = f // tn
    w_spec = pl.BlockSpec((1, k, tn), lambda i, j: (j // per, 0, j % per))
    in_specs = [pl.BlockSpec((tm, k), lambda i, j: (i, 0)), w_spec, w_spec]
    args = [x, wg, wu]
    if gates_t is not None:
        in_specs.append(pl.BlockSpec((1, tm, 1), lambda i, j: (j // per, i, 0)))
        args.append(gates_t)
    return pl.pallas_call(
        functools.partial(_glu_body, has_gate=gates_t is not None),
        out_shape=jax.ShapeDtypeStruct((m, e * f), BF16),
        grid=(m // tm, e * per),
        in_specs=in_specs,
        out_specs=pl.BlockSpec((tm, tn), lambda i, j: (i, j)),
        compiler_params=_params(("parallel", "parallel"), 56),
        name=name,
    )(*args)


def _ada_body(c_ref, w_ref, b_ref, o_ref):
    a = _silu(c_ref[...]).astype(BF16)
    o_ref[...] = _dot(a, w_ref[...].astype(BF16)) + b_ref[...]


def _ada_mod(c_pad, w_mod, b_mod):
    rows, d = c_pad.shape
    n = w_mod.shape[1]
    tn = 512
    return pl.pallas_call(
        _ada_body,
        out_shape=jax.ShapeDtypeStruct((rows, n), F32),
        grid=(n // tn,),
        in_specs=[pl.BlockSpec((rows, d), lambda j: (0, 0)),
                  pl.BlockSpec((d, tn), lambda j: (0, j)),
                  pl.BlockSpec((1, tn), lambda j: (0, j))],
        out_specs=pl.BlockSpec((rows, tn), lambda j: (0, j)),
        compiler_params=_params(("parallel",), 40),
        name="ada_mod",
    )(c_pad, w_mod, b_mod.reshape(1, n))


def _norm_mod_f32(x_ref, g_ref, sc_ref, sh_ref):
    x = x_ref[0]
    r = lax.rsqrt(jnp.mean(x * x, axis=-1, keepdims=True) + EPS)
    return (x * r * g_ref[...]) * (1.0 + sc_ref[0]) + sh_ref[0]


def _norm_body(x_ref, g_ref, sc_ref, sh_ref, o_ref):
    o_ref[0] = _norm_mod_f32(x_ref, g_ref, sc_ref, sh_ref).astype(o_ref.dtype)


def _norm_router_body(x_ref, g_ref, sc_ref, sh_ref, wr_ref, o_ref, gt_ref, *, n_exp):
    h = _norm_mod_f32(x_ref, g_ref, sc_ref, sh_ref)
    o_ref[0] = h.astype(o_ref.dtype)
    logits = jnp.dot(h, wr_ref[...], precision=lax.Precision.HIGHEST, preferred_element_type=F32)
    lane = lax.broadcasted_iota(jnp.int32, logits.shape, 1).astype(F32)
    neg = -jnp.inf
    l1 = jnp.where(lane < n_exp, logits, neg)
    m1 = jnp.max(l1, axis=-1, keepdims=True)
    i1 = jnp.min(jnp.where(l1 == m1, lane, float(LANES)), axis=-1, keepdims=True)
    l2 = jnp.where(lane == i1, neg, l1)
    m2 = jnp.max(l2, axis=-1, keepdims=True)
    i2 = jnp.min(jnp.where(l2 == m2, lane, float(LANES)), axis=-1, keepdims=True)
    e = jnp.exp(m2 - m1)
    p1 = 1.0 / (1.0 + e)
    p2 = e / (1.0 + e)
    gt_ref[0] = jnp.where(lane == i1, p1, 0.0) + jnp.where(lane == i2, p2, 0.0)


def _final_norm_body(x_ref, g_ref, o_ref):
    x = x_ref[0]
    r = lax.rsqrt(jnp.mean(x * x, axis=-1, keepdims=True) + EPS)
    o_ref[0] = x * r * g_ref[...]


def _norm_mod(x3, g, sc, sh, w_router_pad=None, n_exp=0):
    b, s, d = x3.shape
    ts = min(256, s)
    row = pl.BlockSpec((1, ts, d), lambda bb, i: (bb, i, 0))
    vec = pl.BlockSpec((1, 1, d), lambda bb, i: (bb, 0, 0))
    in_specs = [row, pl.BlockSpec((1, d), lambda bb, i: (0, 0)), vec, vec]
    args = [x3, g.reshape(1, d), sc, sh]
    if w_router_pad is None:
        return pl.pallas_call(
            _norm_body, out_shape=jax.ShapeDtypeStruct((b, s, d), BF16), grid=(b, s // ts),
            in_specs=in_specs, out_specs=row,
            compiler_params=_params(("parallel", "parallel"), 40), name="norm_mod")(*args)
    in_specs.append(pl.BlockSpec((d, LANES), lambda bb, i: (0, 0)))
    args.append(w_router_pad)
    return pl.pallas_call(
        functools.partial(_norm_router_body, n_exp=n_exp),
        out_shape=(jax.ShapeDtypeStruct((b, s, d), BF16), jax.ShapeDtypeStruct((b, s, LANES), F32)),
        grid=(b, s // ts), in_specs=in_specs,
        out_specs=(row, pl.BlockSpec((1, ts, LANES), lambda bb, i: (bb, i, 0))),
        compiler_params=_params(("parallel", "parallel"), 40), name="norm_mod_router")(*args)


def _final_norm(x3, g):
    b, s, d = x3.shape
    ts = min(256, s)
    row = pl.BlockSpec((1, ts, d), lambda bb, i: (bb, i, 0))
    return pl.pallas_call(
        _final_norm_body, out_shape=jax.ShapeDtypeStruct((b, s, d), F32), grid=(b, s // ts),
        in_specs=[row, pl.BlockSpec((1, d), lambda bb, i: (0, 0))], out_specs=row,
        compiler_params=_params(("parallel", "parallel"), 40), name="final_norm")(x3, g.reshape(1, d))


def _dft_chan_body(u_ref, w_ref, o_ref):
    r = _dot(u_ref[0], w_ref[...])
    gd = r.shape[1] // 2
    o_ref[0, 0] = r[:, :gd].astype(o_ref.dtype)
    o_ref[0, 1] = r[:, gd:].astype(o_ref.dtype)


def _dft_seq_body(a_ref, z_ref, o_ref, acc_ref, *, nk, scale):
    k = pl.program_id(3)
    d = _dot(a_ref[...], z_ref[0])

    @pl.when(k == 0)
    def _():
        acc_ref[...] = d

    @pl.when(k > 0)
    def _():
        acc_ref[...] += d

    @pl.when(k == nk - 1)
    def _():
        o_ref[0] = (acc_ref[...] * scale).astype(o_ref.dtype)


def _dft_tables(s, gd):
    def cs(n):
        i = jnp.arange(n, dtype=jnp.int32)
        ang = ((i[:, None] * i[None, :]) % n).astype(F32) * (2.0 * math.pi / n)
        return jnp.cos(ang), jnp.sin(ang)
    cc, sc = cs(gd)
    cp, sp = cs(s)
    return (jnp.concatenate([cc, sc], axis=1).astype(BF16),
            jnp.concatenate([cp, -sp], axis=1).astype(BF16))


def _fourier_mix(proj3, a_width):
    b, s, _ = proj3.shape
    gd = a_width // A_GROUPS
    w_chan, w_pos = _dft_tables(s, gd)
    ts = min(1024, s)
    z = pl.pallas_call(
        _dft_chan_body,
        out_shape=jax.ShapeDtypeStruct((b, 2, s, a_width), BF16),
        grid=(b, s // ts, A_GROUPS),
        in_specs=[pl.BlockSpec((1, ts, gd), lambda bb, i, g: (bb, i, g)),
                  pl.BlockSpec((gd, 2 * gd), lambda bb, i, g: (0, 0))],
        out_specs=pl.BlockSpec((1, 2, ts, gd), lambda bb, i, g: (bb, 0, i, g)),
        compiler_params=_params(("parallel", "parallel", "parallel"), 40),
        name="dft_channels",
    )(proj3, w_chan)
    z = z.reshape(b, 2 * s, a_width)
    tm, tn, tk = min(1024, s), min(1024, a_width), min(2048, 2 * s)
    nk = 2 * s // tk
    return pl.pallas_call(
        functools.partial(_dft_seq_body, nk=nk, scale=1.0 / math.sqrt(s * gd)),
        out_shape=jax.ShapeDtypeStruct((b, s, a_width), BF16),
        grid=(b, s // tm, a_width // tn, nk),
        in_specs=[pl.BlockSpec((tm, tk), lambda bb, i, j, k: (i, k)),
                  pl.BlockSpec((1, tk, tn), lambda bb, i, j, k: (bb, k, j))],
        out_specs=pl.BlockSpec((1, tm, tn), lambda bb, i, j, k: (bb, i, j)),
        scratch_shapes=[pltpu.VMEM((tm, tn), F32)],
        compiler_params=_params(("parallel", "parallel", "parallel", "arbitrary"), 40),
        name="dft_positions",
    )(w_pos, z)


def _rotate(x, cos, sin, half):
    x1, x2 = x[:, :half], x[:, half:]
    return x1 * cos - x2 * sin, x1 * sin + x2 * cos


def _ret_state_body(lg_ref, k_ref, v_ref, cos_ref, sin_ref, sb_ref, s_ref, *, chunk, heads, half, qk_scale):
    h, c = pl.program_id(1), pl.program_id(2)

    @pl.when(c == 0)
    def _():
        s_ref[...] = jnp.zeros_like(s_ref)

    sb_ref[0, 0, 0] = s_ref[...].astype(sb_ref.dtype)
    lgb = lg_ref[heads + h]
    k1, k2 = _rotate(k_ref[0].astype(F32), cos_ref[...], sin_ref[...], half)
    j = lax.broadcasted_iota(jnp.int32, (chunk, 1), 0).astype(F32)
    wj = jnp.exp(j * lgb) * qk_scale
    kd_t = jnp.concatenate([k1 * wj, k2 * wj], axis=1).T.astype(BF16)
    carry = jnp.exp(jnp.full((1, s_ref.shape[1]), float(chunk), F32) * lgb)
    s_ref[...] = s_ref[...] * carry + _dot(kd_t, v_ref[0])


def _ret_main_body(lg_ref, q_ref, k_ref, v_ref, g_ref, cos_ref, sin_ref, sb_ref, y_ref, s_ref, d_ref,
                   *, chunk, heads, half, qk_scale):
    h, c = pl.program_id(1), pl.program_id(2)
    lgf, lgb = lg_ref[h], lg_ref[heads + h]

    @pl.when(c == 0)
    def _():
        s_ref[...] = jnp.zeros_like(s_ref)
        i = lax.broadcasted_iota(jnp.int32, (chunk, chunk), 0)
        j = lax.broadcasted_iota(jnp.int32, (chunk, chunk), 1)
        d = (i - j).astype(F32)
        d_ref[...] = jnp.exp(jnp.where(d >= 0.0, d * lgf, -d * lgb))

    cos, sin = cos_ref[...], sin_ref[...]
    q1, q2 = _rotate(q_ref[0].astype(F32), cos, sin, half)
    k1, k2 = _rotate(k_ref[0].astype(F32), cos, sin, half)
    k1, k2 = k1 * qk_scale, k2 * qk_scale
    v = v_ref[0]
    qr = jnp.concatenate([q1, q2], axis=1)
    kr = jnp.concatenate([k1, k2], axis=1)
    scores = _dot_nt(qr.astype(BF16), kr.astype(BF16))
    o = _dot((scores * d_ref[...]).astype(BF16), v)
    ii = lax.broadcasted_iota(jnp.int32, (chunk, 1), 0).astype(F32)
    ef = jnp.exp((ii + 1.0) * lgf)
    eb = jnp.exp((float(chunk) - ii) * lgb)
    qq = jnp.concatenate([qr * ef, qr * eb], axis=1).astype(BF16)
    st = jnp.concatenate([s_ref[...].astype(BF16), sb_ref[0, 0, 0]], axis=0)
    o = o + _dot(qq, st)
    wj = jnp.exp((float(chunk - 1) - ii) * lgf)
    kd_t = (kr * wj).T.astype(BF16)
    carry = jnp.exp(jnp.full((1, s_ref.shape[1]), float(chunk), F32) * lgf)
    s_ref[...] = s_ref[...] * carry + _dot(kd_t, v)
    r = lax.rsqrt(jnp.mean(o * o, axis=-1, keepdims=True) + EPS)
    y_ref[0] = (o * r * _silu(g_ref[0].astype(F32))).astype(y_ref.dtype)


def _retention_mix(proj3, a_width, qk, vd):
    b, s, _ = proj3.shape
    heads, half = B_HEADS, qk // 2
    chunk = min(RET_CHUNK, s)
    nc = s // chunk
    q0 = a_width // qk
    k0 = q0 + heads
    v0 = (a_width + 2 * heads * qk) // vd
    g0 = v0 + heads
    inv = 1.0 / (ROPE_BASE ** jnp.linspace(0.0, 1.0, half, dtype=F32))
    ang = jnp.arange(s, dtype=F32)[:, None] * inv[None, :]
    cos, sin = jnp.cos(ang), jnp.sin(ang)
    hh = jnp.arange(heads, dtype=F32)
    lg = jnp.concatenate([jnp.log1p(-jnp.exp2(-5.0 - hh)), jnp.log1p(-jnp.exp2(-5.5 - hh))])
    qk_scale = float(qk) ** -0.5
    smem = pl.BlockSpec(memory_space=pltpu.SMEM)

    def rev(c):
        return nc - 1 - c

    sb = pl.pallas_call(
        functools.partial(_ret_state_body, chunk=chunk, heads=heads, half=half, qk_scale=qk_scale),
        out_shape=jax.ShapeDtypeStruct((b, heads, nc, qk, vd), BF16),
        grid=(b, heads, nc),
        in_specs=[smem,
                  pl.BlockSpec((1, chunk, qk), lambda bb, h, c: (bb, rev(c), k0 + h)),
                  pl.BlockSpec((1, chunk, vd), lambda bb, h, c: (bb, rev(c), v0 + h)),
                  pl.BlockSpec((chunk, half), lambda bb, h, c: (rev(c), 0)),
                  pl.BlockSpec((chunk, half), lambda bb, h, c: (rev(c), 0))],
        out_specs=pl.BlockSpec((1, 1, 1, qk, vd), lambda bb, h, c: (bb, h, rev(c), 0, 0)),
        scratch_shapes=[pltpu.VMEM((qk, vd), F32)],
        compiler_params=_params(("parallel", "parallel", "arbitrary"), 40),
        name="retention_bwd_states",
    )(lg, proj3, proj3, cos, sin)
    return pl.pallas_call(
        functools.partial(_ret_main_body, chunk=chunk, heads=heads, half=half, qk_scale=qk_scale),
        out_shape=jax.ShapeDtypeStruct((b, s, heads * vd), BF16),
        grid=(b, heads, nc),
        in_specs=[smem,
                  pl.BlockSpec((1, chunk, qk), lambda bb, h, c: (bb, c, q0 + h)),
                  pl.BlockSpec((1, chunk, qk), lambda bb, h, c: (bb, c, k0 + h)),
                  pl.BlockSpec((1, chunk, vd), lambda bb, h, c: (bb, c, v0 + h)),
                  pl.BlockSpec((1, chunk, vd), lambda bb, h, c: (bb, c, g0 + h)),
                  pl.BlockSpec((chunk, half), lambda bb, h, c: (c, 0)),
                  pl.BlockSpec((chunk, half), lambda bb, h, c: (c, 0)),
                  pl.BlockSpec((1, 1, 1, qk, vd), lambda bb, h, c: (bb, h, c, 0, 0))],
        out_specs=pl.BlockSpec((1, chunk, vd), lambda bb, h, c: (bb, c, h)),
        scratch_shapes=[pltpu.VMEM((qk, vd), F32), pltpu.VMEM((chunk, chunk), F32)],
        compiler_params=_params(("parallel", "parallel", "arbitrary"), 40),
        name="retention_main",
    )(lg, proj3, proj3, proj3, proj3, cos, sin, sb)


def _conv_rows(ext_ref, cur_ref, prev_ref, next_ref, cw_ref, cb_ref, blk, nblk, ts):
    hr = BF16_ROWS
    prev = prev_ref[0].astype(F32)
    nxt = next_ref[0].astype(F32)
    ext_ref[0:hr, :] = jnp.where(blk > 0, prev, 0.0)
    ext_ref[hr:hr + ts, :] = cur_ref[0].astype(F32)
    ext_ref[hr + ts:hr + ts + hr, :] = jnp.where(blk < nblk - 1, nxt, 0.0)
    acc = cb_ref[...] + cw_ref[0:1, :] * ext_ref[pl.ds(hr - 2, ts), :]
    for t in range(1, CONV_K):
        acc = acc + cw_ref[t:t + 1, :] * ext_ref[pl.ds(hr - 2 + t, ts), :]
    return acc


def _conv_silu_body(cur_ref, prev_ref, next_ref, cw_ref, cb_ref, o_ref, ext_ref, *, ts, nblk):
    blk = pl.program_id(1)
    xc = _conv_rows(ext_ref, cur_ref, prev_ref, next_ref, cw_ref, cb_ref, blk, nblk, ts)
    o_ref[0] = _silu(xc).astype(o_ref.dtype)


def _conv_silu(proj3, col0, width, cw, cb):
    b, s, _ = proj3.shape
    ts, tc = min(512, s), 1024
    nblk, per, last = s // ts, ts // BF16_ROWS, s // BF16_ROWS - 1
    c0 = col0 // tc
    return pl.pallas_call(
        functools.partial(_conv_silu_body, ts=ts, nblk=nblk),
        out_shape=jax.ShapeDtypeStruct((b, s, width), BF16),
        grid=(b, nblk, width // tc),
        in_specs=[pl.BlockSpec((1, ts, tc), lambda bb, i, j: (bb, i, c0 + j)),
                  pl.BlockSpec((1, BF16_ROWS, tc), lambda bb, i, j: (bb, jnp.maximum(i * per - 1, 0), c0 + j)),
                  pl.BlockSpec((1, BF16_ROWS, tc), lambda bb, i, j: (bb, jnp.minimum((i + 1) * per, last), c0 + j)),
                  pl.BlockSpec((CONV_K, tc), lambda bb, i, j: (0, j)),
                  pl.BlockSpec((1, tc), lambda bb, i, j: (0, j))],
        out_specs=pl.BlockSpec((1, ts, tc), lambda bb, i, j: (bb, i, j)),
        scratch_shapes=[pltpu.VMEM((ts + 2 * BF16_ROWS, tc), F32)],
        compiler_params=_params(("parallel", "parallel", "parallel"), 40),
        name="ssd_conv_silu",
    )(proj3, proj3, proj3, cw, cb.reshape(1, width))


def _lru_body(*refs, ts, nblk, reverse, bdim):
    if reverse:
        (cur_ref, prev_ref, next_ref, cw_ref, cb_ref, wa_ref, ba_ref, wx_ref, bx_ref, lam_ref,
         hf_ref, gc_ref, o_ref, ext_ref, a_ref, b_ref, h_ref) = refs
    else:
        (cur_ref, prev_ref, next_ref, cw_ref, cb_ref, wa_ref, ba_ref, wx_ref, bx_ref, lam_ref,
         o_ref, ext_ref, a_ref, b_ref, h_ref) = refs
    i = pl.program_id(1)
    blk = nblk - 1 - i if reverse else i

    @pl.when(i == 0)
    def _():
        h_ref[...] = jnp.zeros_like(h_ref)

    xc = _conv_rows(ext_ref, cur_ref, prev_ref, next_ref, cw_ref, cb_ref, blk, nblk, ts)
    rate = -LRU_C * _softplus(-lam_ref[0])
    for n in range(C_BLOCKS):
        cols = slice(n * bdim, (n + 1) * bdim)
        xn = xc[:, cols]
        xb = xn.astype(BF16)
        r = _sigmoid(_dot(xb, wa_ref[0, n]) + ba_ref[0][:, cols])
        ig = _sigmoid(_dot(xb, wx_ref[0, n]) + bx_ref[0][:, cols])
        a = jnp.exp(rate[:, cols] * r)
        a_ref[:, cols] = a
        b_ref[:, cols] = jnp.sqrt(1.0 - a * a) * (ig * xn)

    def step(t, h):
        tt = ts - 1 - t if reverse else t
        h = a_ref[pl.ds(tt, 1), :] * h + b_ref[pl.ds(tt, 1), :]
        b_ref[pl.ds(tt, 1), :] = h
        return h

    h_ref[...] = lax.fori_loop(0, ts, step, h_ref[...], unroll=8)
    if reverse:
        o_ref[0] = ((hf_ref[0] + b_ref[...]) * _gelu_tanh(gc_ref[0].astype(F32))).astype(o_ref.dtype)
    else:
        o_ref[0] = b_ref[...]


def _lru_pass(proj3, cw, cb, wa, ba, wx, bx, lam, direction, hf=None):
    b, s, _ = proj3.shape
    cwid = cw.shape[1]
    bdim = cwid // C_BLOCKS
    ts = min(256, s)
    nblk, per, last = s // ts, ts // BF16_ROWS, s // BF16_ROWS - 1
    reverse = direction == 1

    def blk(i):
        return nblk - 1 - i if reverse else i

    vec = pl.BlockSpec((1, 1, cwid), lambda bb, i: (direction, 0, 0))
    wblk = pl.BlockSpec((1, C_BLOCKS, bdim, bdim), lambda bb, i: (direction, 0, 0, 0))
    in_specs = [pl.BlockSpec((1, ts, cwid), lambda bb, i: (bb, blk(i), 1)),
                pl.BlockSpec((1, BF16_ROWS, cwid), lambda bb, i: (bb, jnp.maximum(blk(i) * per - 1, 0), 1)),
                pl.BlockSpec((1, BF16_ROWS, cwid), lambda bb, i: (bb, jnp.minimum((blk(i) + 1) * per, last), 1)),
                pl.BlockSpec((CONV_K, cwid), lambda bb, i: (0, 0)),
                pl.BlockSpec((1, cwid), lambda bb, i: (0, 0)),
                wblk, vec, wblk, vec, vec]
    args = [proj3, proj3, proj3, cw, cb.reshape(1, cwid), wa, ba.reshape(2, 1, cwid), wx,
            bx.reshape(2, 1, cwid), lam.reshape(2, 1, cwid)]
    row = pl.BlockSpec((1, ts, cwid), lambda bb, i: (bb, blk(i), 0))
    if reverse:
        in_specs += [row, row]
        args += [hf, proj3]
        out_dtype = BF16
    else:
        out_dtype = F32
    return pl.pallas_call(
        functools.partial(_lru_body, ts=ts, nblk=nblk, reverse=reverse, bdim=bdim),
        out_shape=jax.ShapeDtypeStruct((b, s, cwid), out_dtype),
        grid=(b, nblk),
        in_specs=in_specs,
        out_specs=row,
        scratch_shapes=[pltpu.VMEM((ts + 2 * BF16_ROWS, cwid), F32), pltpu.VMEM((ts, cwid), F32),
                        pltpu.VMEM((ts, cwid), F32), pltpu.VMEM((1, cwid), F32)],
        compiler_params=_params(("parallel", "arbitrary"), 48),
        name="rglru_bwd" if reverse else "rglru_fwd",
    )(*args)


def _cumsum_rows(x):
    n = x.shape[0]
    tri = (lax.broadcasted_iota(jnp.int32, (n, n), 0) >= lax.broadcasted_iota(jnp.int32, (n, n), 1))
    tri = jnp.where(tri, 1.0, 0.0).astype(BF16)
    hi = x.astype(BF16)
    r1 = x - hi.astype(F32)
    mid = r1.astype(BF16)
    lo = (r1 - mid.astype(F32)).astype(BF16)
    return _dot(tri, hi) + _dot(tri, mid) + _dot(tri, lo)


def _ssd_decays(dt_ref, bias_ref, alog_ref):
    dt = _softplus(dt_ref[0] + bias_ref[0])
    la = -dt * jnp.exp(alog_ref[0])
    cs = _cumsum_rows(la)
    return dt, la, cs


def _pair_cols(col_lo, col_hi, lane_lo):
    return jnp.where(lane_lo, col_lo, col_hi)


def _ssd_state_update(s, bm, xs32, wcols, tots, lane_lo):
    xw, carry = [], []
    for p in range(D_HPG // 2):
        cols = slice(p * LANES, (p + 1) * LANES)
        xw.append(xs32[:, cols] * _pair_cols(wcols[2 * p], wcols[2 * p + 1], lane_lo))
        carry.append(_pair_cols(tots[2 * p], tots[2 * p + 1], lane_lo[0:1, :]))
    xw = jnp.concatenate(xw, axis=1).astype(BF16)
    carry = jnp.exp(jnp.concatenate(carry, axis=1))
    bm_t = bm.astype(F32).T.astype(BF16)
    return s * carry + _dot(bm_t, xw)


def _ssd_state_body(xs_ref, bm_ref, dt_ref, bias_ref, alog_ref, sbo_ref, s_ref, *, chunk):
    c, g = pl.program_id(1), pl.program_id(2)

    @pl.when(c == 0)
    def _():
        s_ref[g] = jnp.zeros(s_ref.shape[1:], F32)

    s = s_ref[g]
    sbo_ref[0, 0, 0] = s.astype(sbo_ref.dtype)
    dt, la, cs = _ssd_decays(dt_ref, bias_ref, alog_ref)
    cbx = cs - la
    lane_lo = lax.broadcasted_iota(jnp.int32, (chunk, LANES), 1) < D_HEADDIM
    wcols = [jnp.exp(cbx[:, D_HPG + h:D_HPG + h + 1]) * dt[:, D_HPG + h:D_HPG + h + 1] for h in range(D_HPG)]
    tots = [cs[chunk - 1:chunk, D_HPG + h:D_HPG + h + 1] for h in range(D_HPG)]
    s_ref[g] = _ssd_state_update(s, bm_ref[0], xs_ref[0].astype(F32), wcols, tots, lane_lo)


def _ssd_main_body(xs_ref, bm_ref, cm_ref, z_ref, dt_ref, bias_ref, alog_ref, dsk_ref, nw_ref, sb_ref,
                   y_ref, s_ref, *, chunk):
    c, g = pl.program_id(1), pl.program_id(2)

    @pl.when(c == 0)
    def _():
        s_ref[g] = jnp.zeros(s_ref.shape[1:], F32)

    dt, la, cs = _ssd_decays(dt_ref, bias_ref, alog_ref)
    cbx = cs - la
    cs_t, cbx_t, dt_t = cs.T, cbx.T, dt.T
    bm, cm, xs = bm_ref[0], cm_ref[0], xs_ref[0]
    xs32 = xs.astype(F32)
    cm32 = cm.astype(F32)
    scores = _dot_nt(cm, bm)
    causal = (lax.broadcasted_iota(jnp.int32, (chunk, chunk), 0)
              >= lax.broadcasted_iota(jnp.int32, (chunk, chunk), 1))
    lane_lo = lax.broadcasted_iota(jnp.int32, (chunk, LANES), 1) < D_HEADDIM
    s = s_ref[g]
    s_bf = s.astype(BF16)
    sb_bf = sb_ref[0, 0, 0]
    nst = s.shape[0]
    keep_lo = lax.broadcasted_iota(jnp.int32, (chunk + 2 * nst, LANES), 1) < D_HEADDIM
    ys, wcols, tots = [], [], []
    for p in range(D_HPG // 2):
        cols = slice(p * LANES, (p + 1) * LANES)
        rhs = jnp.concatenate([xs[:, cols], s_bf[:, cols], sb_bf[:, cols]], axis=0)
        acc = None
        for q in range(2):
            hf = 2 * p + q
            hb = D_HPG + hf
            cf_col = cs[:, hf:hf + 1]
            cb_col = cbx[:, hb:hb + 1]
            arg = jnp.where(causal, cf_col - cs_t[hf:hf + 1, :], cbx_t[hb:hb + 1, :] - cb_col)
            dsel = jnp.where(causal, dt_t[hf:hf + 1, :], dt_t[hb:hb + 1, :])
            w = (scores * jnp.exp(arg) * dsel).astype(BF16)
            tot_f = cs[chunk - 1:chunk, hf:hf + 1]
            tot_b = cs[chunk - 1:chunk, hb:hb + 1]
            ef = jnp.exp(cf_col)
            eb = jnp.exp(tot_b - cb_col)
            lhs = jnp.concatenate([w, (cm32 * ef).astype(BF16), (cm32 * eb).astype(BF16)], axis=1)
            keep = keep_lo if q == 0 else jnp.logical_not(keep_lo)
            part = _dot(lhs, jnp.where(keep, rhs, jnp.zeros_like(rhs)))
            acc = part if acc is None else acc + part
            wcols.append(jnp.exp(tot_f - cf_col) * dt[:, hf:hf + 1])
            tots.append(tot_f)
        ys.append(acc)
    y = jnp.concatenate(ys, axis=1)
    s_ref[g] = _ssd_state_update(s, bm, xs32, wcols, tots, lane_lo)
    y = (y + xs32 * dsk_ref[0]) * _silu(z_ref[0].astype(F32))
    r = lax.rsqrt(jnp.mean(y * y, axis=-1, keepdims=True) + EPS)
    y_ref[0] = (y * r * nw_ref[0]).astype(y_ref.dtype)


def _ssd_mix(proj3, z_col0, xbc_act, dt_pad, bias_pad, alog_pad, d_skip, norm_w):
    b, s, _ = xbc_act.shape
    gw = D_HPG * D_HEADDIM
    inner = D_GROUPS * gw
    chunk = min(SSD_CHUNK, s)
    nc = s // chunk
    b0 = inner // D_STATE
    c0 = b0 + D_GROUPS
    z0 = z_col0 // gw

    def rev(c):
        return nc - 1 - c

    gvec = pl.BlockSpec((1, 1, LANES), lambda bb, c, g: (g, 0, 0))
    sb = pl.pallas_call(
        functools.partial(_ssd_state_body, chunk=chunk),
        out_shape=jax.ShapeDtypeStruct((b, nc, D_GROUPS, D_STATE, gw), BF16),
        grid=(b, nc, D_GROUPS),
        in_specs=[pl.BlockSpec((1, chunk, gw), lambda bb, c, g: (bb, rev(c), g)),
                  pl.BlockSpec((1, chunk, D_STATE), lambda bb, c, g: (bb, rev(c), b0 + g)),
                  pl.BlockSpec((1, chunk, LANES), lambda bb, c, g: (bb, rev(c), g)),
                  gvec, gvec],
        out_specs=pl.BlockSpec((1, 1, 1, D_STATE, gw), lambda bb, c, g: (bb, rev(c), g, 0, 0)),
        scratch_shapes=[pltpu.VMEM((D_GROUPS, D_STATE, gw), F32)],
        compiler_params=_params(("parallel", "arbitrary", "arbitrary"), 40),
        name="ssd_bwd_states",
    )(xbc_act, xbc_act, dt_pad, bias_pad, alog_pad)
    gwide = pl.BlockSpec((1, 1, gw), lambda bb, c, g: (g, 0, 0))
    return pl.pallas_call(
        functools.partial(_ssd_main_body, chunk=chunk),
        out_shape=jax.ShapeDtypeStruct((b, s, inner), BF16),
        grid=(b, nc, D_GROUPS),
        in_specs=[pl.BlockSpec((1, chunk, gw), lambda bb, c, g: (bb, c, g)),
                  pl.BlockSpec((1, chunk, D_STATE), lambda bb, c, g: (bb, c, b0 + g)),
                  pl.BlockSpec((1, chunk, D_STATE), lambda bb, c, g: (bb, c, c0 + g)),
                  pl.BlockSpec((1, chunk, gw), lambda bb, c, g: (bb, c, z0 + g)),
                  pl.BlockSpec((1, chunk, LANES), lambda bb, c, g: (bb, c, g)),
                  gvec, gvec, gwide, gwide,
                  pl.BlockSpec((1, 1, 1, D_STATE, gw), lambda bb, c, g: (bb, c, g, 0, 0))],
        out_specs=pl.BlockSpec((1, chunk, gw), lambda bb, c, g: (bb, c, g)),
        scratch_shapes=[pltpu.VMEM((D_GROUPS, D_STATE, gw), F32)],
        compiler_params=_params(("parallel", "arbitrary", "arbitrary"), 40),
        name="ssd_main",
    )(xbc_act, xbc_act, xbc_act, proj3, dt_pad, bias_pad, alog_pad,
      d_skip.reshape(D_GROUPS, 1, gw), norm_w.reshape(D_GROUPS, 1, gw), sb)


def _pad_heads(fwd, bwd):
    lead = fwd.shape[:-1]
    f = fwd.reshape(lead + (D_GROUPS, D_HPG))
    bk = bwd.reshape(lead + (D_GROUPS, D_HPG))
    pad = jnp.zeros(lead + (D_GROUPS, LANES - 2 * D_HPG), fwd.dtype)
    return jnp.concatenate([f, bk, pad], axis=-1).reshape(lead + (D_GROUPS * LANES,))


def _mod_vectors(c, w_mod, b_mod):
    bsz, d = c.shape
    c_pad = jnp.zeros((8, d), F32).at[:bsz].set(c)
    m = _ada_mod(c_pad, w_mod, b_mod)[:bsz]
    return [m[:, i * d:(i + 1) * d].reshape(bsz, 1, d) for i in range(6)]


def _even_layer(x3, mods, p):
    b, s, d = x3.shape
    sh1, sc1, gt1, sh2, sc2, gt2 = mods
    m = b * s
    h = _norm_mod(x3, p["g_mix_e"], sc1, sh1).reshape(m, d)
    proj = _matmul(h, p["w_in_e"], out_dtype=BF16, name="in_proj_even")
    proj3 = proj.reshape(b, s, -1)
    a_width = d // 2
    qk = d // 16
    y_a = _fourier_mix(proj3, a_width)
    y_b = _retention_mix(proj3, a_width, qk, 2 * qk)
    x2 = _matmul(y_a.reshape(m, -1), p["w_out_e"], x2=y_b.reshape(m, -1), res=x3.reshape(m, d), gate=gt1,
                 rows_per_gate=s, out_dtype=F32, name="out_proj_even")
    h = _norm_mod(x2.reshape(b, s, d), p["g_ffn_e"], sc2, sh2).reshape(m, d)
    a = _glu_up(h, p["w_gate_e"], p["w_up_e"], None, name="ffn_up")
    x2 = _matmul(a, p["w_down_e"], res=x2, gate=gt2, rows_per_gate=s, out_dtype=F32, name="ffn_down")
    return x2.reshape(b, s, d)


def _odd_layer(x3, mods, p):
    b, s, d = x3.shape
    sh1, sc1, gt1, sh2, sc2, gt2 = mods
    m = b * s
    h = _norm_mod(x3, p["g_mix_o"], sc1, sh1).reshape(m, d)
    proj3 = _matmul(h, p["w_in_o_main"], out_dtype=BF16, name="in_proj_odd").reshape(b, s, -1)
    dt_pad = _matmul(h, p["w_in_o_dt"], out_dtype=F32, name="in_proj_dt").reshape(b, s, -1)
    cwid = d // 2
    hf = _lru_pass(proj3, p["conv_w_c"], p["conv_b_c"], p["lru_wa"], p["lru_ba"], p["lru_wx"], p["lru_bx"],
                   p["lru_lam"], 0)
    y_c = _lru_pass(proj3, p["conv_w_c"], p["conv_b_c"], p["lru_wa"], p["lru_ba"], p["lru_wx"], p["lru_bx"],
                    p["lru_lam"], 1, hf=hf)
    xbc0 = 2 * cwid + d
    xbc_act = _conv_silu(proj3, xbc0, p["conv_w_d"].shape[1], p["conv_w_d"], p["conv_b_d"])
    y_d = _ssd_mix(proj3, 2 * cwid, xbc_act, dt_pad, p["dt_bias_pad"], p["a_log_pad"], p["d_skip"],
                   p["ssd_norm_w"])
    x2 = _matmul(y_c.reshape(m, -1), p["w_out_o"], x2=y_d.reshape(m, -1), res=x3.reshape(m, d), gate=gt1,
                 rows_per_gate=s, out_dtype=F32, name="out_proj_odd")
    n_exp = p["w_gate_x"].shape[0]
    h, gates = _norm_mod(x2.reshape(b, s, d), p["g_ffn_o"], sc2, sh2, p["w_router_pad"], n_exp)
    gates_t = gates.reshape(m, LANES)[:, :n_exp].T.reshape(n_exp, m, 1)
    a = _glu_up(h.reshape(m, d), p["w_gate_x"], p["w_up_x"], gates_t, name="moe_up")
    x2 = _matmul(a, p["w_down_x"], res=x2, gate=gt2, rows_per_gate=s, out_dtype=F32, name="moe_down")
    return x2.reshape(b, s, d)


def _trunk(x3, c, p):
    mods_e = _mod_vectors(c, p["w_mod_e"], p["b_mod_e"])
    mods_o = _mod_vectors(c, p["w_mod_o"], p["b_mod_o"])
    x3 = _even_layer(x3, mods_e, p)
    x3 = _odd_layer(x3, mods_o, p)
    return _final_norm(x3, p["g_final"])


def kernel(x_prompt, x_sample, c_prompt, c_sample, w_mod_e, b_mod_e, g_mix_e, g_ffn_e, w_in_e, w_out_e, w_gate_e, w_up_e, w_down_e, w_mod_o, b_mod_o, g_mix_o, g_ffn_o, w_in_o, conv_w_c, conv_b_c, lru_wa, lru_ba, lru_wx, lru_bx, lru_lam, conv_w_d, conv_b_d, ssd_a_log, ssd_dt_bias, ssd_d, ssd_norm_w, w_out_o, w_router, w_gate_x, w_up_x, w_down_x, g_final):
    d = x_prompt.shape[-1]
    n_main = w_in_o.shape[2] - 2 * D_GROUPS * D_HPG
    heads = D_GROUPS * D_HPG
    w_in_o0 = w_in_o[0]
    n_exp = w_router.shape[2]
    p = dict(
        w_mod_e=w_mod_e[0], b_mod_e=b_mod_e[0], g_mix_e=g_mix_e[0], g_ffn_e=g_ffn_e[0],
        w_in_e=w_in_e[0].astype(BF16), w_out_e=w_out_e[0].astype(BF16),
        w_gate_e=w_gate_e.astype(BF16), w_up_e=w_up_e.astype(BF16), w_down_e=w_down_e[0].astype(BF16),
        w_mod_o=w_mod_o[0], b_mod_o=b_mod_o[0], g_mix_o=g_mix_o[0], g_ffn_o=g_ffn_o[0],
        w_in_o_main=w_in_o0[:, :n_main].astype(BF16),
        w_in_o_dt=_pad_heads(w_in_o0[:, n_main:n_main + heads], w_in_o0[:, n_main + heads:]).astype(BF16),
        conv_w_c=conv_w_c[0], conv_b_c=conv_b_c[0],
        lru_wa=lru_wa[0].astype(BF16), lru_ba=lru_ba[0], lru_wx=lru_wx[0].astype(BF16), lru_bx=lru_bx[0],
        lru_lam=lru_lam[0],
        conv_w_d=conv_w_d[0], conv_b_d=conv_b_d[0],
        dt_bias_pad=_pad_heads(ssd_dt_bias[0, 0], ssd_dt_bias[0, 1]).reshape(D_GROUPS, 1, LANES),
        a_log_pad=_pad_heads(ssd_a_log[0, 0], ssd_a_log[0, 1]).reshape(D_GROUPS, 1, LANES),
        d_skip=jnp.repeat(ssd_d[0], D_HEADDIM), ssd_norm_w=ssd_norm_w[0],
        w_out_o=w_out_o[0].astype(BF16),
        w_router_pad=jnp.zeros((d, LANES), F32).at[:, :n_exp].set(w_router[0]),
        w_gate_x=w_gate_x[0].astype(BF16), w_up_x=w_up_x[0].astype(BF16),
        w_down_x=w_down_x[0].reshape(-1, d).astype(BF16),
        g_final=g_final,
    )
    return (_trunk(x_prompt, c_prompt, p), _trunk(x_sample, c_sample, p))
```

```python
import functools
import math

import jax
import jax.numpy as jnp
from jax import lax
from jax.experimental import pallas as pl
from jax.experimental.pallas import tpu as pltpu

F32 = jnp.float32
BF16 = jnp.bfloat16

EPS = 1e-6
ROPE_BASE = 10000.0
LRU_C = 8.0
CONV_K = 4
A_GROUPS = 4
B_HEADS = 8
C_BLOCKS = 8
D_HEADDIM = 64
D_STATE = 128
D_GROUPS = 8
D_HPG = 8
TOP_K = 2

LANES = 128
BF16_ROWS = 16
SSD_CHUNK = 128
SSD_GROUPS_PER_STEP = 2
RET_CHUNK = 512
MOE_TILE = 512
MM_TILE = 1024
MM_TK_MAX = 4096


def _params(sem, vmem_mb):
    return pltpu.CompilerParams(dimension_semantics=sem, vmem_limit_bytes=vmem_mb << 20)


def _dot(a, b):
    return jnp.dot(a, b, preferred_element_type=F32)


def _dot_nt(a, b):
    return lax.dot_general(a, b, (((1,), (1,)), ((), ())), preferred_element_type=F32)


def _sigmoid(x):
    return 1.0 / (1.0 + jnp.exp(-x))


def _silu(x):
    return x / (1.0 + jnp.exp(-x))


def _softplus(x):
    return jnp.maximum(x, 0.0) + jnp.log(1.0 + jnp.exp(-jnp.abs(x)))


def _gelu_tanh(x):
    return 0.5 * x * (1.0 + jnp.tanh(math.sqrt(2.0 / math.pi) * (x + 0.044715 * (x * x * x))))


def _mm_body(*refs, nk, nk1, has_x2, resid):
    refs = list(refs)
    x_ref = refs.pop(0)
    x2_ref = refs.pop(0) if has_x2 else None
    w_ref = refs.pop(0)
    res_ref = refs.pop(0) if resid else None
    gate_ref = refs.pop(0) if resid else None
    o_ref = refs.pop(0)
    acc_ref = refs.pop(0) if nk > 1 else None

    def finish(acc):
        if resid:
            o_ref[...] = res_ref[...] + gate_ref[0] * acc
        else:
            o_ref[...] = acc.astype(o_ref.dtype)

    if nk == 1:
        finish(_dot(x_ref[...], w_ref[...]))
        return

    k = pl.program_id(2)

    @pl.when(k == 0)
    def _():
        acc_ref[...] = _dot(x_ref[...], w_ref[...])

    if has_x2:
        if nk1 > 1:
            @pl.when(jnp.logical_and(k > 0, k < nk1))
            def _():
                acc_ref[...] += _dot(x_ref[...], w_ref[...])

        @pl.when(k >= nk1)
        def _():
            acc_ref[...] += _dot(x2_ref[...], w_ref[...])
    else:
        @pl.when(k > 0)
        def _():
            acc_ref[...] += _dot(x_ref[...], w_ref[...])

    @pl.when(k == nk - 1)
    def _():
        finish(acc_ref[...])


def _matmul(x, w, *, out_dtype, name, x2=None, res=None, gate=None, rows_per_gate=None, tk=None):
    m, k1 = x.shape
    k, n = w.shape
    tm, tn = min(MM_TILE, m), min(MM_TILE, n)
    if tk is None:
        tk = k if k <= MM_TK_MAX else MM_TK_MAX // 2
    if x2 is not None:
        tk = min(tk, math.gcd(k1, x2.shape[1]))
    nk, nk1 = k // tk, k1 // tk
    resid = res is not None
    assert not resid or rows_per_gate % tm == 0, "a row tile must not straddle two gate rows"
    in_specs = [pl.BlockSpec((tm, tk), lambda i, j, kk: (i, jnp.minimum(kk, nk1 - 1)))]
    args = [x]
    if x2 is not None:
        in_specs.append(pl.BlockSpec((tm, tk), lambda i, j, kk: (i, jnp.maximum(kk - nk1, 0))))
        args.append(x2)
    in_specs.append(pl.BlockSpec((tk, tn), lambda i, j, kk: (kk, j)))
    args.append(w)
    if resid:
        in_specs.append(pl.BlockSpec((tm, tn), lambda i, j, kk: (i, j)))
        in_specs.append(pl.BlockSpec((1, 1, tn), lambda i, j, kk: ((i * tm) // rows_per_gate, 0, j)))
        args += [res, gate]
    scratch = [pltpu.VMEM((tm, tn), F32)] if nk > 1 else []
    return pl.pallas_call(
        functools.partial(_mm_body, nk=nk, nk1=nk1, has_x2=x2 is not None, resid=resid),
        out_shape=jax.ShapeDtypeStruct((m, n), out_dtype),
        grid=(m // tm, n // tn, nk),
        in_specs=in_specs,
        out_specs=pl.BlockSpec((tm, tn), lambda i, j, kk: (i, j)),
        scratch_shapes=scratch,
        compiler_params=_params(("parallel", "parallel", "arbitrary"), 56),
        name=name,
    )(*args)


def _glu_body(*refs, has_gate):
    if has_gate:
        x_ref, wg_ref, wu_ref, gcol_ref, o_ref = refs
    else:
        x_ref, wg_ref, wu_ref, o_ref = refs
    x = x_ref[...]
    a = _dot(x, wg_ref[0])
    u = _dot(x, wu_ref[0])
    h = _silu(a) * u
    if has_gate:
        h = h * gcol_ref[0]
    o_ref[...] = h.astype(o_ref.dtype)


def _glu_up(x, wg, wu, gates_t, *, name):
    m, k = x.shape
    e, _, f = wg.shape
    tm, tn = min(MM_TILE, m), min(MM_TILE // 2, f)
    per = f // tn
    w_spec = pl.BlockSpec((1, k, tn), lambda i, j: (j // per, 0, j % per))
    in_specs = [pl.BlockSpec((tm, k), lambda i, j: (i, 0)), w_spec, w_spec]
    args = [x, wg, wu]
    if gates_t is not None:
        in_specs.append(pl.BlockSpec((1, tm, 1), lambda i, j: (j // per, i, 0)))
        args.append(gates_t)
    return pl.pallas_call(
        functools.partial(_glu_body, has_gate=gates_t is not None),
        out_shape=jax.ShapeDtypeStruct((m, e * f), BF16),
        grid=(m // tm, e * per),
        in_specs=in_specs,
        out_specs=pl.BlockSpec((tm, tn), lambda i, j: (i, j)),
        compiler_params=_params(("parallel", "parallel"), 56),
        name=name,
    )(*args)


def _ada_body(c_ref, w_ref, b_ref, o_ref):
    a = _silu(c_ref[...]).astype(BF16)
    o_ref[...] = _dot(a, w_ref[...].astype(BF16)) + b_ref[...]


def _ada_mod(c_pad, w_mod, b_mod):
    rows, d = c_pad.shape
    n = w_mod.shape[1]
    tn = 512
    return pl.pallas_call(
        _ada_body,
        out_shape=jax.ShapeDtypeStruct((rows, n), F32),
        grid=(n // tn,),
        in_specs=[pl.BlockSpec((rows, d), lambda j: (0, 0)),
                  pl.BlockSpec((d, tn), lambda j: (0, j)),
                  pl.BlockSpec((1, tn), lambda j: (0, j))],
        out_specs=pl.BlockSpec((rows, tn), lambda j: (0, j)),
        compiler_params=_params(("parallel",), 40),
        name="ada_mod",
    )(c_pad, w_mod, b_mod.reshape(1, n))


def _norm_mod_f32(x_ref, g_ref, sc_ref, sh_ref):
    x = x_ref[0]
    r = lax.rsqrt(jnp.mean(x * x, axis=-1, keepdims=True) + EPS)
    return (x * r * g_ref[...]) * (1.0 + sc_ref[0]) + sh_ref[0]


def _norm_body(x_ref, g_ref, sc_ref, sh_ref, o_ref):
    o_ref[0] = _norm_mod_f32(x_ref, g_ref, sc_ref, sh_ref).astype(o_ref.dtype)


def _norm_router_body(x_ref, g_ref, sc_ref, sh_ref, wr_ref, o_ref, gt_ref, *, n_exp):
    h = _norm_mod_f32(x_ref, g_ref, sc_ref, sh_ref)
    o_ref[0] = h.astype(o_ref.dtype)
    logits = jnp.dot(h, wr_ref[...], precision=lax.Precision.HIGHEST, preferred_element_type=F32)
    lane = lax.broadcasted_iota(jnp.int32, logits.shape, 1).astype(F32)
    neg = -jnp.inf
    l1 = jnp.where(lane < n_exp, logits, neg)
    m1 = jnp.max(l1, axis=-1, keepdims=True)
    i1 = jnp.min(jnp.where(l1 == m1, lane, float(LANES)), axis=-1, keepdims=True)
    l2 = jnp.where(lane == i1, neg, l1)
    m2 = jnp.max(l2, axis=-1, keepdims=True)
    i2 = jnp.min(jnp.where(l2 == m2, lane, float(LANES)), axis=-1, keepdims=True)
    e = jnp.exp(m2 - m1)
    p1 = 1.0 / (1.0 + e)
    p2 = e / (1.0 + e)
    gt_ref[0] = (jnp.where(lane == 0.0, i1, 0.0) + jnp.where(lane == 1.0, i2, 0.0)
                 + jnp.where(lane == 2.0, p1, 0.0) + jnp.where(lane == 3.0, p2, 0.0))


def _combine_norm_body(x_ref, y0_ref, y1_ref, gate_ref, g_ref, o_ref):
    x = x_ref[0] + gate_ref[0] * (y0_ref[...] + y1_ref[...])
    r = lax.rsqrt(jnp.mean(x * x, axis=-1, keepdims=True) + EPS)
    o_ref[0] = x * r * g_ref[...]


def _norm_mod(x3, g, sc, sh, w_router_pad=None, n_exp=0):
    b, s, d = x3.shape
    ts = min(256, s)
    row = pl.BlockSpec((1, ts, d), lambda bb, i: (bb, i, 0))
    vec = pl.BlockSpec((1, 1, d), lambda bb, i: (bb, 0, 0))
    in_specs = [row, pl.BlockSpec((1, d), lambda bb, i: (0, 0)), vec, vec]
    args = [x3, g.reshape(1, d), sc, sh]
    if w_router_pad is None:
        return pl.pallas_call(
            _norm_body, out_shape=jax.ShapeDtypeStruct((b, s, d), BF16), grid=(b, s // ts),
            in_specs=in_specs, out_specs=row,
            compiler_params=_params(("parallel", "parallel"), 40), name="norm_mod")(*args)
    in_specs.append(pl.BlockSpec((d, LANES), lambda bb, i: (0, 0)))
    args.append(w_router_pad)
    return pl.pallas_call(
        functools.partial(_norm_router_body, n_exp=n_exp),
        out_shape=(jax.ShapeDtypeStruct((b, s, d), F32), jax.ShapeDtypeStruct((b, s, LANES), F32)),
        grid=(b, s // ts), in_specs=in_specs,
        out_specs=(row, pl.BlockSpec((1, ts, LANES), lambda bb, i: (bb, i, 0))),
        compiler_params=_params(("parallel", "parallel"), 40), name="norm_mod_router")(*args)


def _combine_norm(x3, y2, gate, g):
    b, s, d = x3.shape
    ts = min(256, s)
    nb = s // ts
    row = pl.BlockSpec((1, ts, d), lambda bb, i: (bb, i, 0))
    return pl.pallas_call(
        _combine_norm_body, out_shape=jax.ShapeDtypeStruct((b, s, d), F32), grid=(b, nb),
        in_specs=[row,
                  pl.BlockSpec((ts, d), lambda bb, i: (bb * nb + i, 0)),
                  pl.BlockSpec((ts, d), lambda bb, i: ((b + bb) * nb + i, 0)),
                  pl.BlockSpec((1, 1, d), lambda bb, i: (bb, 0, 0)),
                  pl.BlockSpec((1, d), lambda bb, i: (0, 0))],
        out_specs=row,
        compiler_params=_params(("parallel", "parallel"), 48), name="moe_combine_norm")(x3, y2, y2, gate, g.reshape(1, d))


def _moe_dispatch(routing, n_exp, tile):
    m = routing.shape[0]
    a_tot = 2 * m
    p_tot = a_tot + n_exp * tile
    e_a = jnp.concatenate([routing[:, 0], routing[:, 1]]).astype(jnp.int32)
    p_a = jnp.concatenate([routing[:, 2], routing[:, 3]])
    order = jnp.argsort(e_a, stable=True).astype(jnp.int32)
    cnt = jnp.sum((e_a[:, None] == jnp.arange(n_exp, dtype=jnp.int32)[None, :]).astype(jnp.int32), axis=0)
    cnt_pad = ((cnt + tile - 1) // tile) * tile
    end_u, end_p = jnp.cumsum(cnt), jnp.cumsum(cnt_pad)
    off_u, off_p = end_u - cnt, end_p - cnt_pad
    tile_start = jnp.arange(p_tot // tile, dtype=jnp.int32) * tile
    tile_exp = jnp.minimum(jnp.sum((tile_start[:, None] >= end_p[None, :]).astype(jnp.int32), axis=1), n_exp - 1)
    tile_valid = (tile_start < end_p[-1]).astype(jnp.int32)
    r = jnp.arange(p_tot, dtype=jnp.int32)
    te = tile_exp[r // tile]
    j = r - off_p[te]
    valid = jnp.logical_and(j < cnt[te], tile_valid[r // tile] == 1)
    a_r = order[jnp.clip(off_u[te] + j, 0, a_tot - 1)]
    src_tok = jnp.where(valid, a_r % m, 0).astype(jnp.int32)
    pad_ord = jnp.cumsum(jnp.logical_not(valid).astype(jnp.int32)) - 1
    dst_row = jnp.where(valid, a_r, a_tot + pad_ord).astype(jnp.int32)
    row_gate = jnp.where(valid, p_a[a_r], 0.0).reshape(p_tot, 1)
    return src_tok, dst_row, row_gate, tile_exp.astype(jnp.int32), tile_valid


def _moe_up_body(src_ref, texp_ref, tval_ref, h_hbm, wg_ref, wu_ref, rg_ref, o_ref, x32_ref, xb_ref, sem,
                 *, tile):
    t, jf = pl.program_id(0), pl.program_id(1)
    valid = tval_ref[t] == 1

    def row_copy(r):
        tok = src_ref[t * tile + r]
        return pltpu.make_async_copy(h_hbm.at[pl.ds(tok, 1), :], x32_ref.at[pl.ds(r, 1), :], sem)

    @pl.when(jnp.logical_and(jf == 0, valid))
    def _():
        def start(r, c):
            row_copy(r).start()
            return c

        def wait(r, c):
            row_copy(r).wait()
            return c

        lax.fori_loop(0, tile, start, 0)
        lax.fori_loop(0, tile, wait, 0)
        xb_ref[...] = x32_ref[...].astype(BF16)

    @pl.when(valid)
    def _():
        x = xb_ref[...]
        a = _dot(x, wg_ref[0])
        u = _dot(x, wu_ref[0])
        o_ref[...] = (_silu(a) * u * rg_ref[...]).astype(o_ref.dtype)

    @pl.when(jnp.logical_not(valid))
    def _():
        o_ref[...] = jnp.zeros_like(o_ref)


def _moe_down_body(dst_ref, texp_ref, tval_ref, a_ref, wd_ref, out_hbm, y_ref, sem, *, tile):
    t = pl.program_id(0)

    def row_copy(r):
        row = dst_ref[t * tile + r]
        return pltpu.make_async_copy(y_ref.at[pl.ds(r, 1), :], out_hbm.at[pl.ds(row, 1), :], sem)

    @pl.when(tval_ref[t] == 1)
    def _():
        y_ref[...] = _dot(a_ref[...], wd_ref[0])

    @pl.when(tval_ref[t] != 1)
    def _():
        y_ref[...] = jnp.zeros_like(y_ref)

    def start(r, c):
        row_copy(r).start()
        return c

    def wait(r, c):
        row_copy(r).wait()
        return c

    lax.fori_loop(0, tile, start, 0)
    lax.fori_loop(0, tile, wait, 0)


def _moe_experts(h32, routing, wg, wu, wd):
    m, d = h32.shape
    n_exp, _, f = wg.shape
    tile = min(MOE_TILE, m)
    src_tok, dst_row, row_gate, tile_exp, tile_valid = _moe_dispatch(routing, n_exp, tile)
    p_tot = src_tok.shape[0]
    nt = p_tot // tile
    tn = min(MM_TILE // 2, f)
    nf = f // tn
    a = pl.pallas_call(
        functools.partial(_moe_up_body, tile=tile),
        out_shape=jax.ShapeDtypeStruct((p_tot, f), BF16),
        grid_spec=pltpu.PrefetchScalarGridSpec(
            num_scalar_prefetch=3,
            grid=(nt, nf),
            in_specs=[pl.BlockSpec(memory_space=pl.ANY),
                      pl.BlockSpec((1, d, tn), lambda t, j, src, te, tv: (te[t], 0, j)),
                      pl.BlockSpec((1, d, tn), lambda t, j, src, te, tv: (te[t], 0, j)),
                      pl.BlockSpec((tile, 1), lambda t, j, src, te, tv: (t, 0))],
            out_specs=pl.BlockSpec((tile, tn), lambda t, j, src, te, tv: (t, j)),
            scratch_shapes=[pltpu.VMEM((tile, d), F32), pltpu.VMEM((tile, d), BF16), pltpu.SemaphoreType.DMA]),
        compiler_params=_params(("arbitrary", "arbitrary"), 48),
        name="moe_up",
    )(src_tok, tile_exp, tile_valid, h32, wg, wu, row_gate)
    return pl.pallas_call(
        functools.partial(_moe_down_body, tile=tile),
        out_shape=jax.ShapeDtypeStruct((p_tot, d), F32),
        grid_spec=pltpu.PrefetchScalarGridSpec(
            num_scalar_prefetch=3,
            grid=(nt,),
            in_specs=[pl.BlockSpec((tile, f), lambda t, dst, te, tv: (t, 0)),
                      pl.BlockSpec((1, f, d), lambda t, dst, te, tv: (te[t], 0, 0))],
            out_specs=pl.BlockSpec(memory_space=pl.ANY),
            scratch_shapes=[pltpu.VMEM((tile, d), F32), pltpu.SemaphoreType.DMA]),
        compiler_params=_params(("arbitrary",), 48),
        name="moe_down",
    )(dst_row, tile_exp, tile_valid, a, wd)


def _dft_chan_body(u_ref, w_ref, o_ref):
    r = _dot(u_ref[0], w_ref[...])
    gd = r.shape[1] // 2
    o_ref[0, 0] = r[:, :gd].astype(o_ref.dtype)
    o_ref[0, 1] = r[:, gd:].astype(o_ref.dtype)


def _dft_outer_body(x_ref, f_ref, tc_ref, ts_ref, o_ref):
    r = _dot(f_ref[...], x_ref[0])
    n2 = r.shape[0] // 2
    yr, yi = r[:n2], r[n2:]
    tc, ts = tc_ref[0], ts_ref[0]
    o_ref[0, 0] = (yr * tc + yi * ts).astype(o_ref.dtype)
    o_ref[0, 1] = (yi * tc - yr * ts).astype(o_ref.dtype)


def _dft_inner_body(y_ref, f_ref, o_ref, *, kb, cols, scale):
    for kk in range(kb):
        rhs = jnp.concatenate([y_ref[0, 0, kk], y_ref[0, 1, kk]], axis=0)
        o_ref[0, :, kk * cols:(kk + 1) * cols] = (_dot(f_ref[...], rhs) * scale).astype(o_ref.dtype)


def _cos_sin(rows, cols, period):
    ang = ((rows[:, None] * cols[None, :]) % period).astype(F32) * (2.0 * math.pi / period)
    return jnp.cos(ang), jnp.sin(ang)


def _fourier_mix(proj3, a_width):
    b, s, _ = proj3.shape
    gd = a_width // A_GROUPS
    n2 = LANES
    n1 = s // n2
    i_gd, i1, i2 = (jnp.arange(n, dtype=jnp.int32) for n in (gd, n1, n2))
    cc, sc = _cos_sin(i_gd, i_gd, gd)
    w_chan = jnp.concatenate([cc, sc], axis=1).astype(BF16)
    c2, s2 = _cos_sin(i2, i2, n2)
    f_outer = jnp.concatenate([jnp.concatenate([c2, -s2], axis=1),
                               jnp.concatenate([-s2, -c2], axis=1)], axis=0).astype(BF16)
    tw_c, tw_s = _cos_sin(i1, i2, s)
    c1, s1 = _cos_sin(i1, i1, n1)
    f_inner = jnp.concatenate([c1, s1], axis=1).astype(BF16)
    ts = min(1024, s)
    z = pl.pallas_call(
        _dft_chan_body,
        out_shape=jax.ShapeDtypeStruct((b, 2, s, a_width), BF16),
        grid=(b, s // ts, A_GROUPS),
        in_specs=[pl.BlockSpec((1, ts, gd), lambda bb, i, g: (bb, i, g)),
                  pl.BlockSpec((gd, 2 * gd), lambda bb, i, g: (0, 0))],
        out_specs=pl.BlockSpec((1, 2, ts, gd), lambda bb, i, g: (bb, 0, i, g)),
        compiler_params=_params(("parallel", "parallel", "parallel"), 40),
        name="dft_channels",
    )(proj3, w_chan)
    y = pl.pallas_call(
        _dft_outer_body,
        out_shape=jax.ShapeDtypeStruct((b, 2, n2, n1 * a_width), BF16),
        grid=(b, n1),
        in_specs=[pl.BlockSpec((1, 2 * n2, a_width), lambda bb, j: (bb, 0, j)),
                  pl.BlockSpec((2 * n2, 2 * n2), lambda bb, j: (0, 0)),
                  pl.BlockSpec((1, n2, 1), lambda bb, j: (j, 0, 0)),
                  pl.BlockSpec((1, n2, 1), lambda bb, j: (j, 0, 0))],
        out_specs=pl.BlockSpec((1, 2, n2, a_width), lambda bb, j: (bb, 0, 0, j)),
        compiler_params=_params(("parallel", "parallel"), 40),
        name="dft_positions_outer",
    )(z.reshape(b, 2 * n2, n1 * a_width), f_outer, tw_c.reshape(n1, n2, 1), tw_s.reshape(n1, n2, 1))
    kb = 8
    out = pl.pallas_call(
        functools.partial(_dft_inner_body, kb=kb, cols=a_width, scale=1.0 / math.sqrt(s * gd)),
        out_shape=jax.ShapeDtypeStruct((b, n1, n2 * a_width), BF16),
        grid=(b, n2 // kb),
        in_specs=[pl.BlockSpec((1, 2, kb, n1, a_width), lambda bb, k: (bb, 0, k, 0, 0)),
                  pl.BlockSpec((n1, 2 * n1), lambda bb, k: (0, 0))],
        out_specs=pl.BlockSpec((1, n1, kb * a_width), lambda bb, k: (bb, 0, k)),
        compiler_params=_params(("parallel", "parallel"), 40),
        name="dft_positions_inner",
    )(y.reshape(b, 2, n2, n1, a_width), f_inner)
    return out.reshape(b, s, a_width)


def _rotate(x, cos, sin, half):
    x1, x2 = x[:, :half], x[:, half:]
    return x1 * cos - x2 * sin, x1 * sin + x2 * cos


def _ret_state_body(lg_ref, k_ref, v_ref, cos_ref, sin_ref, sb_ref, s_ref, *, chunk, heads, half, qk_scale):
    h, c = pl.program_id(1), pl.program_id(2)

    @pl.when(c == 0)
    def _():
        s_ref[...] = jnp.zeros_like(s_ref)

    sb_ref[0, 0, 0] = s_ref[...].astype(sb_ref.dtype)
    lgb = lg_ref[heads + h]
    k1, k2 = _rotate(k_ref[0].astype(F32), cos_ref[...], sin_ref[...], half)
    j = lax.broadcasted_iota(jnp.int32, (chunk, 1), 0).astype(F32)
    wj = jnp.exp(j * lgb) * qk_scale
    kd_t = jnp.concatenate([k1 * wj, k2 * wj], axis=1).T.astype(BF16)
    carry = jnp.exp(jnp.full((1, s_ref.shape[1]), float(chunk), F32) * lgb)
    s_ref[...] = s_ref[...] * carry + _dot(kd_t, v_ref[0])


def _ret_main_body(lg_ref, q_ref, k_ref, v_ref, g_ref, cos_ref, sin_ref, sb_ref, y_ref, s_ref, d_ref,
                   *, chunk, heads, half, qk_scale):
    h, c = pl.program_id(1), pl.program_id(2)
    lgf, lgb = lg_ref[h], lg_ref[heads + h]

    @pl.when(c == 0)
    def _():
        s_ref[...] = jnp.zeros_like(s_ref)
        i = lax.broadcasted_iota(jnp.int32, (chunk, chunk), 0)
        j = lax.broadcasted_iota(jnp.int32, (chunk, chunk), 1)
        d = (i - j).astype(F32)
        d_ref[...] = jnp.exp(jnp.where(d >= 0.0, d * lgf, -d * lgb))

    cos, sin = cos_ref[...], sin_ref[...]
    q1, q2 = _rotate(q_ref[0].astype(F32), cos, sin, half)
    k1, k2 = _rotate(k_ref[0].astype(F32), cos, sin, half)
    k1, k2 = k1 * qk_scale, k2 * qk_scale
    v = v_ref[0]
    qr = jnp.concatenate([q1, q2], axis=1)
    kr = jnp.concatenate([k1, k2], axis=1)
    scores = _dot_nt(qr.astype(BF16), kr.astype(BF16))
    o = _dot((scores * d_ref[...]).astype(BF16), v)
    ii = lax.broadcasted_iota(jnp.int32, (chunk, 1), 0).astype(F32)
    ef = jnp.exp((ii + 1.0) * lgf)
    eb = jnp.exp((float(chunk) - ii) * lgb)
    qq = jnp.concatenate([qr * ef, qr * eb], axis=1).astype(BF16)
    st = jnp.concatenate([s_ref[...].astype(BF16), sb_ref[0, 0, 0]], axis=0)
    o = o + _dot(qq, st)
    wj = jnp.exp((float(chunk - 1) - ii) * lgf)
    kd_t = (kr * wj).T.astype(BF16)
    carry = jnp.exp(jnp.full((1, s_ref.shape[1]), float(chunk), F32) * lgf)
    s_ref[...] = s_ref[...] * carry + _dot(kd_t, v)
    r = lax.rsqrt(jnp.mean(o * o, axis=-1, keepdims=True) + EPS)
    y_ref[0] = (o * r * _silu(g_ref[0].astype(F32))).astype(y_ref.dtype)


def _retention_mix(proj3, a_width, qk, vd):
    b, s, _ = proj3.shape
    heads, half = B_HEADS, qk // 2
    chunk = min(RET_CHUNK, s)
    nc = s // chunk
    q0 = a_width // qk
    k0 = q0 + heads
    v0 = (a_width + 2 * heads * qk) // vd
    g0 = v0 + heads
    inv = 1.0 / (ROPE_BASE ** jnp.linspace(0.0, 1.0, half, dtype=F32))
    ang = jnp.arange(s, dtype=F32)[:, None] * inv[None, :]
    cos, sin = jnp.cos(ang), jnp.sin(ang)
    hh = jnp.arange(heads, dtype=F32)
    lg = jnp.concatenate([jnp.log1p(-jnp.exp2(-5.0 - hh)), jnp.log1p(-jnp.exp2(-5.5 - hh))])
    qk_scale = float(qk) ** -0.5
    smem = pl.BlockSpec(memory_space=pltpu.SMEM)

    def rev(c):
        return nc - 1 - c

    sb = pl.pallas_call(
        functools.partial(_ret_state_body, chunk=chunk, heads=heads, half=half, qk_scale=qk_scale),
        out_shape=jax.ShapeDtypeStruct((b, heads, nc, qk, vd), BF16),
        grid=(b, heads, nc),
        in_specs=[smem,
                  pl.BlockSpec((1, chunk, qk), lambda bb, h, c: (bb, rev(c), k0 + h)),
                  pl.BlockSpec((1, chunk, vd), lambda bb, h, c: (bb, rev(c), v0 + h)),
                  pl.BlockSpec((chunk, half), lambda bb, h, c: (rev(c), 0)),
                  pl.BlockSpec((chunk, half), lambda bb, h, c: (rev(c), 0))],
        out_specs=pl.BlockSpec((1, 1, 1, qk, vd), lambda bb, h, c: (bb, h, rev(c), 0, 0)),
        scratch_shapes=[pltpu.VMEM((qk, vd), F32)],
        compiler_params=_params(("parallel", "parallel", "arbitrary"), 40),
        name="retention_bwd_states",
    )(lg, proj3, proj3, cos, sin)
    return pl.pallas_call(
        functools.partial(_ret_main_body, chunk=chunk, heads=heads, half=half, qk_scale=qk_scale),
        out_shape=jax.ShapeDtypeStruct((b, s, heads * vd), BF16),
        grid=(b, heads, nc),
        in_specs=[smem,
                  pl.BlockSpec((1, chunk, qk), lambda bb, h, c: (bb, c, q0 + h)),
                  pl.BlockSpec((1, chunk, qk), lambda bb, h, c: (bb, c, k0 + h)),
                  pl.BlockSpec((1, chunk, vd), lambda bb, h, c: (bb, c, v0 + h)),
                  pl.BlockSpec((1, chunk, vd), lambda bb, h, c: (bb, c, g0 + h)),
                  pl.BlockSpec((chunk, half), lambda bb, h, c: (c, 0)),
                  pl.BlockSpec((chunk, half), lambda bb, h, c: (c, 0)),
                  pl.BlockSpec((1, 1, 1, qk, vd), lambda bb, h, c: (bb, h, c, 0, 0))],
        out_specs=pl.BlockSpec((1, chunk, vd), lambda bb, h, c: (bb, c, h)),
        scratch_shapes=[pltpu.VMEM((qk, vd), F32), pltpu.VMEM((chunk, chunk), F32)],
        compiler_params=_params(("parallel", "parallel", "arbitrary"), 40),
        name="retention_main",
    )(lg, proj3, proj3, proj3, proj3, cos, sin, sb)


def _conv_rows(ext_ref, cur_ref, prev_ref, next_ref, cw_ref, cb_ref, blk, nblk, ts):
    hr = BF16_ROWS
    prev = prev_ref[0].astype(F32)
    nxt = next_ref[0].astype(F32)
    ext_ref[0:hr, :] = jnp.where(blk > 0, prev, 0.0)
    ext_ref[hr:hr + ts, :] = cur_ref[0].astype(F32)
    ext_ref[hr + ts:hr + ts + hr, :] = jnp.where(blk < nblk - 1, nxt, 0.0)
    acc = cb_ref[...] + cw_ref[0:1, :] * ext_ref[pl.ds(hr - 2, ts), :]
    for t in range(1, CONV_K):
        acc = acc + cw_ref[t:t + 1, :] * ext_ref[pl.ds(hr - 2 + t, ts), :]
    return acc


def _conv_silu_body(cur_ref, prev_ref, next_ref, cw_ref, cb_ref, o_ref, ext_ref, *, ts, nblk):
    blk = pl.program_id(1)
    xc = _conv_rows(ext_ref, cur_ref, prev_ref, next_ref, cw_ref, cb_ref, blk, nblk, ts)
    o_ref[0] = _silu(xc).astype(o_ref.dtype)


def _conv_silu(proj3, col0, width, cw, cb):
    b, s, _ = proj3.shape
    ts, tc = min(512, s), 1024
    nblk, per, last = s // ts, ts // BF16_ROWS, s // BF16_ROWS - 1
    c0 = col0 // tc
    return pl.pallas_call(
        functools.partial(_conv_silu_body, ts=ts, nblk=nblk),
        out_shape=jax.ShapeDtypeStruct((b, s, width), BF16),
        grid=(b, nblk, width // tc),
        in_specs=[pl.BlockSpec((1, ts, tc), lambda bb, i, j: (bb, i, c0 + j)),
                  pl.BlockSpec((1, BF16_ROWS, tc), lambda bb, i, j: (bb, jnp.maximum(i * per - 1, 0), c0 + j)),
                  pl.BlockSpec((1, BF16_ROWS, tc), lambda bb, i, j: (bb, jnp.minimum((i + 1) * per, last), c0 + j)),
                  pl.BlockSpec((CONV_K, tc), lambda bb, i, j: (0, j)),
                  pl.BlockSpec((1, tc), lambda bb, i, j: (0, j))],
        out_specs=pl.BlockSpec((1, ts, tc), lambda bb, i, j: (bb, i, j)),
        scratch_shapes=[pltpu.VMEM((ts + 2 * BF16_ROWS, tc), F32)],
        compiler_params=_params(("parallel", "parallel", "parallel"), 40),
        name="ssd_conv_silu",
    )(proj3, proj3, proj3, cw, cb.reshape(1, width))


def _lru_body(*refs, ts, nblk, reverse, bdim):
    if reverse:
        (cur_ref, prev_ref, next_ref, cw_ref, cb_ref, wa_ref, ba_ref, wx_ref, bx_ref, lam_ref,
         hf_ref, gc_ref, o_ref, ext_ref, a_ref, b_ref, h_ref) = refs
    else:
        (cur_ref, prev_ref, next_ref, cw_ref, cb_ref, wa_ref, ba_ref, wx_ref, bx_ref, lam_ref,
         o_ref, ext_ref, a_ref, b_ref, h_ref) = refs
    i = pl.program_id(1)
    blk = nblk - 1 - i if reverse else i

    @pl.when(i == 0)
    def _():
        h_ref[...] = jnp.zeros_like(h_ref)

    xc = _conv_rows(ext_ref, cur_ref, prev_ref, next_ref, cw_ref, cb_ref, blk, nblk, ts)
    rate = -LRU_C * _softplus(-lam_ref[0])
    for n in range(C_BLOCKS):
        cols = slice(n * bdim, (n + 1) * bdim)
        xn = xc[:, cols]
        xb = xn.astype(BF16)
        r = _sigmoid(_dot(xb, wa_ref[0, n]) + ba_ref[0][:, cols])
        ig = _sigmoid(_dot(xb, wx_ref[0, n]) + bx_ref[0][:, cols])
        a = jnp.exp(rate[:, cols] * r)
        a_ref[:, cols] = a
        b_ref[:, cols] = jnp.sqrt(1.0 - a * a) * (ig * xn)

    def step(t, h):
        tt = ts - 1 - t if reverse else t
        h = a_ref[pl.ds(tt, 1), :] * h + b_ref[pl.ds(tt, 1), :]
        b_ref[pl.ds(tt, 1), :] = h
        return h

    h_ref[...] = lax.fori_loop(0, ts, step, h_ref[...], unroll=8)
    if reverse:
        o_ref[0] = ((hf_ref[0] + b_ref[...]) * _gelu_tanh(gc_ref[0].astype(F32))).astype(o_ref.dtype)
    else:
        o_ref[0] = b_ref[...]


def _lru_pass(proj3, cw, cb, wa, ba, wx, bx, lam, direction, hf=None):
    b, s, _ = proj3.shape
    cwid = cw.shape[1]
    bdim = cwid // C_BLOCKS
    ts = min(256, s)
    nblk, per, last = s // ts, ts // BF16_ROWS, s // BF16_ROWS - 1
    reverse = direction == 1

    def blk(i):
        return nblk - 1 - i if reverse else i

    vec = pl.BlockSpec((1, 1, cwid), lambda bb, i: (direction, 0, 0))
    wblk = pl.BlockSpec((1, C_BLOCKS, bdim, bdim), lambda bb, i: (direction, 0, 0, 0))
    in_specs = [pl.BlockSpec((1, ts, cwid), lambda bb, i: (bb, blk(i), 1)),
                pl.BlockSpec((1, BF16_ROWS, cwid), lambda bb, i: (bb, jnp.maximum(blk(i) * per - 1, 0), 1)),
                pl.BlockSpec((1, BF16_ROWS, cwid), lambda bb, i: (bb, jnp.minimum((blk(i) + 1) * per, last), 1)),
                pl.BlockSpec((CONV_K, cwid), lambda bb, i: (0, 0)),
                pl.BlockSpec((1, cwid), lambda bb, i: (0, 0)),
                wblk, vec, wblk, vec, vec]
    args = [proj3, proj3, proj3, cw, cb.reshape(1, cwid), wa, ba.reshape(2, 1, cwid), wx,
            bx.reshape(2, 1, cwid), lam.reshape(2, 1, cwid)]
    row = pl.BlockSpec((1, ts, cwid), lambda bb, i: (bb, blk(i), 0))
    if reverse:
        in_specs += [row, row]
        args += [hf, proj3]
        out_dtype = BF16
    else:
        out_dtype = F32
    return pl.pallas_call(
        functools.partial(_lru_body, ts=ts, nblk=nblk, reverse=reverse, bdim=bdim),
        out_shape=jax.ShapeDtypeStruct((b, s, cwid), out_dtype),
        grid=(b, nblk),
        in_specs=in_specs,
        out_specs=row,
        scratch_shapes=[pltpu.VMEM((ts + 2 * BF16_ROWS, cwid), F32), pltpu.VMEM((ts, cwid), F32),
                        pltpu.VMEM((ts, cwid), F32), pltpu.VMEM((1, cwid), F32)],
        compiler_params=_params(("parallel", "arbitrary"), 48),
        name="rglru_bwd" if reverse else "rglru_fwd",
    )(*args)


def _cumsum_rows(x):
    n = x.shape[0]
    tri = (lax.broadcasted_iota(jnp.int32, (n, n), 0) >= lax.broadcasted_iota(jnp.int32, (n, n), 1))
    tri = jnp.where(tri, 1.0, 0.0).astype(BF16)
    hi = x.astype(BF16)
    r1 = x - hi.astype(F32)
    mid = r1.astype(BF16)
    lo = (r1 - mid.astype(F32)).astype(BF16)
    return _dot(tri, hi) + _dot(tri, mid) + _dot(tri, lo)


def _ssd_decays(dt_raw, bias, alog):
    dt = _softplus(dt_raw + bias)
    la = -dt * jnp.exp(alog)
    cs = _cumsum_rows(la)
    return dt, la, cs


def _pair_cols(x, lo, lane_lo):
    return jnp.where(lane_lo, x[:, lo:lo + 1], x[:, lo + 1:lo + 2])


def _ssd_state_update(s, bm, xs32, w_all, tot, first, lane_lo):
    xw, carry = [], []
    for p in range(D_HPG // 2):
        cols = slice(p * LANES, (p + 1) * LANES)
        xw.append(xs32[:, cols] * _pair_cols(w_all, first + 2 * p, lane_lo))
        carry.append(_pair_cols(tot, first + 2 * p, lane_lo[0:1, :]))
    xw = jnp.concatenate(xw, axis=1).astype(BF16)
    carry = jnp.exp(jnp.concatenate(carry, axis=1))
    bm_t = bm.astype(F32).T.astype(BF16)
    return s * carry + _dot(bm_t, xw)


def _ssd_state_body(xs_ref, bm_ref, dt_ref, bias_ref, alog_ref, sbo_ref, s_ref, *, chunk, gps):
    c, gi = pl.program_id(1), pl.program_id(2)
    gw = D_HPG * D_HEADDIM
    lane_lo = lax.broadcasted_iota(jnp.int32, (chunk, LANES), 1) < D_HEADDIM
    for gg in range(gps):
        g = gi * gps + gg

        @pl.when(c == 0)
        def _():
            s_ref[g] = jnp.zeros(s_ref.shape[1:], F32)

        s = s_ref[g]
        sbo_ref[0, 0, gg] = s.astype(sbo_ref.dtype)
        dt, la, cs = _ssd_decays(dt_ref[0, :, gg * LANES:(gg + 1) * LANES], bias_ref[gg], alog_ref[gg])
        w_all = jnp.exp(cs - la) * dt
        s_ref[g] = _ssd_state_update(s, bm_ref[0, :, gg * D_STATE:(gg + 1) * D_STATE],
                                     xs_ref[0, :, gg * gw:(gg + 1) * gw].astype(F32),
                                     w_all, cs[chunk - 1:chunk, :], D_HPG, lane_lo)


def _ssd_group(xs, bm, cm, z, dt_raw, bias, alog, dsk, nw, s, sb_bf, chunk, causal, lane_lo):
    dt, la, cs = _ssd_decays(dt_raw, bias, alog)
    cbx = cs - la
    tot = cs[chunk - 1:chunk, :]
    ef_all = jnp.exp(cs)
    eb_all = jnp.exp(tot - cbx)
    wf_all = jnp.exp(tot - cs) * dt
    cs_t, cbx_t, dt_t = cs.T, cbx.T, dt.T
    scores = _dot_nt(cm, bm)
    xs32 = xs.astype(F32)
    y_f = _dot(cm, s.astype(BF16))
    y_b = _dot(cm, sb_bf)
    zeros = jnp.zeros((chunk, LANES), BF16)
    ys = []
    for p in range(D_HPG // 2):
        cols = slice(p * LANES, (p + 1) * LANES)
        xs_p = xs[:, cols]
        ws = []
        for q in range(2):
            hf = 2 * p + q
            hb = D_HPG + hf
            arg = jnp.where(causal, cs[:, hf:hf + 1] - cs_t[hf:hf + 1, :], cbx_t[hb:hb + 1, :] - cbx[:, hb:hb + 1])
            dsel = jnp.where(causal, dt_t[hf:hf + 1, :], dt_t[hb:hb + 1, :])
            ws.append((scores * jnp.exp(arg) * dsel).astype(BF16))
        rhs = jnp.concatenate([jnp.where(lane_lo, xs_p, zeros), jnp.where(lane_lo, zeros, xs_p)], axis=0)
        intra = _dot(jnp.concatenate(ws, axis=1), rhs)
        ys.append(intra + y_f[:, cols] * _pair_cols(ef_all, 2 * p, lane_lo)
                  + y_b[:, cols] * _pair_cols(eb_all, D_HPG + 2 * p, lane_lo))
    y = jnp.concatenate(ys, axis=1)
    s_new = _ssd_state_update(s, bm, xs32, wf_all, tot, 0, lane_lo)
    y = (y + xs32 * dsk) * _silu(z.astype(F32))
    r = lax.rsqrt(jnp.mean(y * y, axis=-1, keepdims=True) + EPS)
    return (y * r * nw).astype(BF16), s_new


def _ssd_main_body(xs_ref, bm_ref, cm_ref, z_ref, dt_ref, bias_ref, alog_ref, dsk_ref, nw_ref, sb_ref,
                   y_ref, s_ref, *, chunk, gps):
    c, gi = pl.program_id(1), pl.program_id(2)
    gw = D_HPG * D_HEADDIM
    causal = (lax.broadcasted_iota(jnp.int32, (chunk, chunk), 0)
              >= lax.broadcasted_iota(jnp.int32, (chunk, chunk), 1))
    lane_lo = lax.broadcasted_iota(jnp.int32, (chunk, LANES), 1) < D_HEADDIM
    for gg in range(gps):
        g = gi * gps + gg

        @pl.when(c == 0)
        def _():
            s_ref[g] = jnp.zeros(s_ref.shape[1:], F32)

        wide = slice(gg * gw, (gg + 1) * gw)
        st = slice(gg * D_STATE, (gg + 1) * D_STATE)
        y, s_new = _ssd_group(xs_ref[0, :, wide], bm_ref[0, :, st], cm_ref[0, :, st], z_ref[0, :, wide],
                              dt_ref[0, :, gg * LANES:(gg + 1) * LANES], bias_ref[gg], alog_ref[gg],
                              dsk_ref[gg], nw_ref[gg], s_ref[g], sb_ref[0, 0, gg], chunk, causal, lane_lo)
        s_ref[g] = s_new
        y_ref[0, :, wide] = y


def _ssd_mix(proj3, z_col0, xbc_act, dt_pad, bias_pad, alog_pad, d_skip, norm_w):
    b, s, _ = xbc_act.shape
    gw = D_HPG * D_HEADDIM
    inner = D_GROUPS * gw
    chunk = min(SSD_CHUNK, s)
    nc = s // chunk
    gps = SSD_GROUPS_PER_STEP
    ng = D_GROUPS // gps
    b0 = inner // (gps * D_STATE)
    c0 = b0 + ng
    z0 = z_col0 // (gps * gw)

    def rev(c):
        return nc - 1 - c

    gvec = pl.BlockSpec((gps, 1, LANES), lambda bb, c, g: (g, 0, 0))
    sb = pl.pallas_call(
        functools.partial(_ssd_state_body, chunk=chunk, gps=gps),
        out_shape=jax.ShapeDtypeStruct((b, nc, D_GROUPS, D_STATE, gw), BF16),
        grid=(b, nc, ng),
        in_specs=[pl.BlockSpec((1, chunk, gps * gw), lambda bb, c, g: (bb, rev(c), g)),
                  pl.BlockSpec((1, chunk, gps * D_STATE), lambda bb, c, g: (bb, rev(c), b0 + g)),
                  pl.BlockSpec((1, chunk, gps * LANES), lambda bb, c, g: (bb, rev(c), g)),
                  gvec, gvec],
        out_specs=pl.BlockSpec((1, 1, gps, D_STATE, gw), lambda bb, c, g: (bb, rev(c), g, 0, 0)),
        scratch_shapes=[pltpu.VMEM((D_GROUPS, D_STATE, gw), F32)],
        compiler_params=_params(("parallel", "arbitrary", "arbitrary"), 40),
        name="ssd_bwd_states",
    )(xbc_act, xbc_act, dt_pad, bias_pad, alog_pad)
    gwide = pl.BlockSpec((gps, 1, gw), lambda bb, c, g: (g, 0, 0))
    return pl.pallas_call(
        functools.partial(_ssd_main_body, chunk=chunk, gps=gps),
        out_shape=jax.ShapeDtypeStruct((b, s, inner), BF16),
        grid=(b, nc, ng),
        in_specs=[pl.BlockSpec((1, chunk, gps * gw), lambda bb, c, g: (bb, c, g)),
                  pl.BlockSpec((1, chunk, gps * D_STATE), lambda bb, c, g: (bb, c, b0 + g)),
                  pl.BlockSpec((1, chunk, gps * D_STATE), lambda bb, c, g: (bb, c, c0 + g)),
                  pl.BlockSpec((1, chunk, gps * gw), lambda bb, c, g: (bb, c, z0 + g)),
                  pl.BlockSpec((1, chunk, gps * LANES), lambda bb, c, g: (bb, c, g)),
                  gvec, gvec, gwide, gwide,
                  pl.BlockSpec((1, 1, gps, D_STATE, gw), lambda bb, c, g: (bb, c, g, 0, 0))],
        out_specs=pl.BlockSpec((1, chunk, gps * gw), lambda bb, c, g: (bb, c, g)),
        scratch_shapes=[pltpu.VMEM((D_GROUPS, D_STATE, gw), F32)],
        compiler_params=_params(("parallel", "arbitrary", "arbitrary"), 40),
        name="ssd_main",
    )(xbc_act, xbc_act, xbc_act, proj3, dt_pad, bias_pad, alog_pad,
      d_skip.reshape(D_GROUPS, 1, gw), norm_w.reshape(D_GROUPS, 1, gw), sb)


def _pad_heads(fwd, bwd):
    lead = fwd.shape[:-1]
    f = fwd.reshape(lead + (D_GROUPS, D_HPG))
    bk = bwd.reshape(lead + (D_GROUPS, D_HPG))
    pad = jnp.zeros(lead + (D_GROUPS, LANES - 2 * D_HPG), fwd.dtype)
    return jnp.concatenate([f, bk, pad], axis=-1).reshape(lead + (D_GROUPS * LANES,))


def _mod_vectors(c_groups, w_mod, b_mod):
    d = c_groups[0].shape[1]
    c_all = jnp.concatenate(c_groups, axis=0)
    rows = c_all.shape[0]
    c_pad = jnp.zeros((-(-rows // 8) * 8, d), F32).at[:rows].set(c_all)
    m = _ada_mod(c_pad, w_mod, b_mod)
    out, lo = [], 0
    for c in c_groups:
        mg = m[lo:lo + c.shape[0]]
        out.append([mg[:, i * d:(i + 1) * d].reshape(c.shape[0], 1, d) for i in range(6)])
        lo += c.shape[0]
    return out


def _even_layer(x3, mods, p):
    b, s, d = x3.shape
    sh1, sc1, gt1, sh2, sc2, gt2 = mods
    m = b * s
    h = _norm_mod(x3, p["g_mix_e"], sc1, sh1).reshape(m, d)
    proj = _matmul(h, p["w_in_e"], out_dtype=BF16, name="in_proj_even")
    proj3 = proj.reshape(b, s, -1)
    a_width = d // 2
    qk = d // 16
    y_a = _fourier_mix(proj3, a_width)
    y_b = _retention_mix(proj3, a_width, qk, 2 * qk)
    x2 = _matmul(y_a.reshape(m, -1), p["w_out_e"], x2=y_b.reshape(m, -1), res=x3.reshape(m, d), gate=gt1,
                 rows_per_gate=s, out_dtype=F32, name="out_proj_even")
    h = _norm_mod(x2.reshape(b, s, d), p["g_ffn_e"], sc2, sh2).reshape(m, d)
    a = _glu_up(h, p["w_gate_e"], p["w_up_e"], None, name="ffn_up")
    x2 = _matmul(a, p["w_down_e"], res=x2, gate=gt2, rows_per_gate=s, out_dtype=F32, name="ffn_down")
    return x2.reshape(b, s, d)


def _odd_layer(x3, mods, p):
    b, s, d = x3.shape
    sh1, sc1, gt1, sh2, sc2, gt2 = mods
    m = b * s
    h = _norm_mod(x3, p["g_mix_o"], sc1, sh1).reshape(m, d)
    proj3 = _matmul(h, p["w_in_o_main"], out_dtype=BF16, name="in_proj_odd").reshape(b, s, -1)
    dt_pad = _matmul(h, p["w_in_o_dt"], out_dtype=F32, name="in_proj_dt").reshape(b, s, -1)
    cwid = d // 2
    hf = _lru_pass(proj3, p["conv_w_c"], p["conv_b_c"], p["lru_wa"], p["lru_ba"], p["lru_wx"], p["lru_bx"],
                   p["lru_lam"], 0)
    y_c = _lru_pass(proj3, p["conv_w_c"], p["conv_b_c"], p["lru_wa"], p["lru_ba"], p["lru_wx"], p["lru_bx"],
                    p["lru_lam"], 1, hf=hf)
    xbc0 = 2 * cwid + d
    xbc_act = _conv_silu(proj3, xbc0, p["conv_w_d"].shape[1], p["conv_w_d"], p["conv_b_d"])
    y_d = _ssd_mix(proj3, 2 * cwid, xbc_act, dt_pad, p["dt_bias_pad"], p["a_log_pad"], p["d_skip"],
                   p["ssd_norm_w"])
    x2 = _matmul(y_c.reshape(m, -1), p["w_out_o"], x2=y_d.reshape(m, -1), res=x3.reshape(m, d), gate=gt1,
                 rows_per_gate=s, out_dtype=F32, name="out_proj_odd")
    n_exp = p["w_gate_x"].shape[0]
    h32, routing = _norm_mod(x2.reshape(b, s, d), p["g_ffn_o"], sc2, sh2, p["w_router_pad"], n_exp)
    y2 = _moe_experts(h32.reshape(m, d), routing.reshape(m, LANES), p["w_gate_x"], p["w_up_x"], p["w_down_x"])
    return _combine_norm(x2.reshape(b, s, d), y2, gt2, p["g_final"])


def _trunk(x3, mods_e, mods_o, p):
    x3 = _even_layer(x3, mods_e, p)
    return _odd_layer(x3, mods_o, p)


def kernel(x_prompt, x_sample, c_prompt, c_sample, w_mod_e, b_mod_e, g_mix_e, g_ffn_e, w_in_e, w_out_e, w_gate_e, w_up_e, w_down_e, w_mod_o, b_mod_o, g_mix_o, g_ffn_o, w_in_o, conv_w_c, conv_b_c, lru_wa, lru_ba, lru_wx, lru_bx, lru_lam, conv_w_d, conv_b_d, ssd_a_log, ssd_dt_bias, ssd_d, ssd_norm_w, w_out_o, w_router, w_gate_x, w_up_x, w_down_x, g_final):
    d = x_prompt.shape[-1]
    n_main = w_in_o.shape[2] - 2 * D_GROUPS * D_HPG
    heads = D_GROUPS * D_HPG
    w_in_o0 = w_in_o[0]
    n_exp = w_router.shape[2]
    p = dict(
        w_mod_e=w_mod_e[0], b_mod_e=b_mod_e[0], g_mix_e=g_mix_e[0], g_ffn_e=g_ffn_e[0],
        w_in_e=w_in_e[0].astype(BF16), w_out_e=w_out_e[0].astype(BF16),
        w_gate_e=w_gate_e.astype(BF16), w_up_e=w_up_e.astype(BF16), w_down_e=w_down_e[0].astype(BF16),
        w_mod_o=w_mod_o[0], b_mod_o=b_mod_o[0], g_mix_o=g_mix_o[0], g_ffn_o=g_ffn_o[0],
        w_in_o_main=w_in_o0[:, :n_main].astype(BF16),
        w_in_o_dt=_pad_heads(w_in_o0[:, n_main:n_main + heads], w_in_o0[:, n_main + heads:]).astype(BF16),
        conv_w_c=conv_w_c[0], conv_b_c=conv_b_c[0],
        lru_wa=lru_wa[0].astype(BF16), lru_ba=lru_ba[0], lru_wx=lru_wx[0].astype(BF16), lru_bx=lru_bx[0],
        lru_lam=lru_lam[0],
        conv_w_d=conv_w_d[0], conv_b_d=conv_b_d[0],
        dt_bias_pad=_pad_heads(ssd_dt_bias[0, 0], ssd_dt_bias[0, 1]).reshape(D_GROUPS, 1, LANES),
        a_log_pad=_pad_heads(ssd_a_log[0, 0], ssd_a_log[0, 1]).reshape(D_GROUPS, 1, LANES),
        d_skip=jnp.repeat(ssd_d[0], D_HEADDIM), ssd_norm_w=ssd_norm_w[0],
        w_out_o=w_out_o[0].astype(BF16),
        w_router_pad=jnp.zeros((d, LANES), F32).at[:, :n_exp].set(w_router[0]),
        w_gate_x=w_gate_x[0].astype(BF16), w_up_x=w_up_x[0].astype(BF16),
        w_down_x=w_down_x[0].astype(BF16),
        g_final=g_final,
    )
    mods_e = _mod_vectors([c_prompt, c_sample], p["w_mod_e"], p["b_mod_e"])
    mods_o = _mod_vectors([c_prompt, c_sample], p["w_mod_o"], p["b_mod_o"])
    return (_trunk(x_prompt, mods_e[0], mods_o[0], p), _trunk(x_sample, mods_e[1], mods_o[1], p))
```

```python
import functools
import math

import jax
import jax.numpy as jnp
from jax import lax
from jax.experimental import pallas as pl
from jax.experimental.pallas import tpu as pltpu

F32 = jnp.float32
BF16 = jnp.bfloat16

EPS = 1e-6
ROPE_BASE = 10000.0
LRU_C = 8.0
CONV_K = 4
A_GROUPS = 4
B_HEADS = 8
C_BLOCKS = 8
D_HEADDIM = 64
D_STATE = 128
D_GROUPS = 8
D_HPG = 8
TOP_K = 2

LANES = 128
BF16_ROWS = 16
SSD_CHUNK = 128
SSD_GROUPS_PER_STEP = 2
RET_CHUNK = 512
MOE_TILE = 512
MM_TILE = 1024
MM_TK_MAX = 4096


def _params(sem, vmem_mb):
    return pltpu.CompilerParams(dimension_semantics=sem, vmem_limit_bytes=vmem_mb << 20)


def _dot(a, b):
    return jnp.dot(a, b, preferred_element_type=F32)


def _dot_nt(a, b):
    return lax.dot_general(a, b, (((1,), (1,)), ((), ())), preferred_element_type=F32)


def _sigmoid(x):
    return 1.0 / (1.0 + jnp.exp(-x))


def _silu(x):
    return x / (1.0 + jnp.exp(-x))


def _softplus(x):
    return jnp.maximum(x, 0.0) + jnp.log(1.0 + jnp.exp(-jnp.abs(x)))


def _gelu_tanh(x):
    return 0.5 * x * (1.0 + jnp.tanh(math.sqrt(2.0 / math.pi) * (x + 0.044715 * (x * x * x))))


def _mm_body(*refs, nk, nk1, has_x2, resid):
    refs = list(refs)
    x_ref = refs.pop(0)
    x2_ref = refs.pop(0) if has_x2 else None
    w_ref = refs.pop(0)
    res_ref = refs.pop(0) if resid else None
    gate_ref = refs.pop(0) if resid else None
    o_ref = refs.pop(0)
    acc_ref = refs.pop(0) if nk > 1 else None

    def finish(acc):
        if resid:
            o_ref[...] = res_ref[...] + gate_ref[0] * acc
        else:
            o_ref[...] = acc.astype(o_ref.dtype)

    if nk == 1:
        finish(_dot(x_ref[...], w_ref[...]))
        return

    k = pl.program_id(2)

    @pl.when(k == 0)
    def _():
        acc_ref[...] = _dot(x_ref[...], w_ref[...])

    if has_x2:
        if nk1 > 1:
            @pl.when(jnp.logical_and(k > 0, k < nk1))
            def _():
                acc_ref[...] += _dot(x_ref[...], w_ref[...])

        @pl.when(k >= nk1)
        def _():
            acc_ref[...] += _dot(x2_ref[...], w_ref[...])
    else:
        @pl.when(k > 0)
        def _():
            acc_ref[...] += _dot(x_ref[...], w_ref[...])

    @pl.when(k == nk - 1)
    def _():
        finish(acc_ref[...])


def _matmul(x, w, *, out_dtype, name, x2=None, res=None, gate=None, rows_per_gate=None, tk=None):
    m, k1 = x.shape
    k, n = w.shape
    tm, tn = min(MM_TILE, m), min(MM_TILE, n)
    if tk is None:
        tk = k if k <= MM_TK_MAX else MM_TK_MAX // 2
    if x2 is not None:
        tk = min(tk, math.gcd(k1, x2.shape[1]))
    nk, nk1 = k // tk, k1 // tk
    resid = res is not None
    assert not resid or rows_per_gate % tm == 0, "a row tile must not straddle two gate rows"
    in_specs = [pl.BlockSpec((tm, tk), lambda i, j, kk: (i, jnp.minimum(kk, nk1 - 1)))]
    args = [x]
    if x2 is not None:
        in_specs.append(pl.BlockSpec((tm, tk), lambda i, j, kk: (i, jnp.maximum(kk - nk1, 0))))
        args.append(x2)
    in_specs.append(pl.BlockSpec((tk, tn), lambda i, j, kk: (kk, j)))
    args.append(w)
    if resid:
        in_specs.append(pl.BlockSpec((tm, tn), lambda i, j, kk: (i, j)))
        in_specs.append(pl.BlockSpec((1, 1, tn), lambda i, j, kk: ((i * tm) // rows_per_gate, 0, j)))
        args += [res, gate]
    scratch = [pltpu.VMEM((tm, tn), F32)] if nk > 1 else []
    return pl.pallas_call(
        functools.partial(_mm_body, nk=nk, nk1=nk1, has_x2=x2 is not None, resid=resid),
        out_shape=jax.ShapeDtypeStruct((m, n), out_dtype),
        grid=(m // tm, n // tn, nk),
        in_specs=in_specs,
        out_specs=pl.BlockSpec((tm, tn), lambda i, j, kk: (i, j)),
        scratch_shapes=scratch,
        compiler_params=_params(("parallel", "parallel", "arbitrary"), 56),
        name=name,
    )(*args)


def _glu_body(*refs, has_gate):
    if has_gate:
        x_ref, wg_ref, wu_ref, gcol_ref, o_ref = refs
    else:
        x_ref, wg_ref, wu_ref, o_ref = refs
    x = x_ref[...]
    a = _dot(x, wg_ref[0])
    u = _dot(x, wu_ref[0])
    h = _silu(a) * u
    if has_gate:
        h = h * gcol_ref[0]
    o_ref[...] = h.astype(o_ref.dtype)


def _glu_up(x, wg, wu, gates_t, *, name):
    m, k = x.shape
    e, _, f = wg.shape
    tm, tn = min(MM_TILE, m), min(MM_TILE // 2, f)
    per = f // tn
    w_spec = pl.BlockSpec((1, k, tn), lambda i, j: (j // per, 0, j % per))
    in_specs = [pl.BlockSpec((tm, k), lambda i, j: (i, 0)), w_spec, w_spec]
    args = [x, wg, wu]
    if gates_t is not None:
        in_specs.append(pl.BlockSpec((1, tm, 1), lambda i, j: (j // per, i, 0)))
        args.append(gates_t)
    return pl.pallas_call(
        functools.partial(_glu_body, has_gate=gates_t is not None),
        out_shape=jax.ShapeDtypeStruct((m, e * f), BF16),
        grid=(m // tm, e * per),
        in_specs=in_specs,
        out_specs=pl.BlockSpec((tm, tn), lambda i, j: (i, j)),
        compiler_params=_params(("parallel", "parallel"), 56),
        name=name,
    )(*args)


def _ada_body(c_ref, w_ref, b_ref, o_ref):
    a = _silu(c_ref[...]).astype(BF16)
    o_ref[...] = _dot(a, w_ref[...].astype(BF16)) + b_ref[...]


def _ada_mod(c_pad, w_mod, b_mod):
    rows, d = c_pad.shape
    n = w_mod.shape[1]
    tn = 512
    return pl.pallas_call(
        _ada_body,
        out_shape=jax.ShapeDtypeStruct((rows, n), F32),
        grid=(n // tn,),
        in_specs=[pl.BlockSpec((rows, d), lambda j: (0, 0)),
                  pl.BlockSpec((d, tn), lambda j: (0, j)),
                  pl.BlockSpec((1, tn), lambda j: (0, j))],
        out_specs=pl.BlockSpec((rows, tn), lambda j: (0, j)),
        compiler_params=_params(("parallel",), 40),
        name="ada_mod",
    )(c_pad, w_mod, b_mod.reshape(1, n))


def _norm_mod_f32(x_ref, g_ref, sc_ref, sh_ref):
    x = x_ref[0]
    r = lax.rsqrt(jnp.mean(x * x, axis=-1, keepdims=True) + EPS)
    return (x * r * g_ref[...]) * (1.0 + sc_ref[0]) + sh_ref[0]


def _norm_body(x_ref, g_ref, sc_ref, sh_ref, o_ref):
    o_ref[0] = _norm_mod_f32(x_ref, g_ref, sc_ref, sh_ref).astype(o_ref.dtype)


def _norm_router_body(x_ref, g_ref, sc_ref, sh_ref, wr_ref, o_ref, gt_ref, *, n_exp):
    h = _norm_mod_f32(x_ref, g_ref, sc_ref, sh_ref)
    o_ref[0] = h.astype(o_ref.dtype)
    logits = jnp.dot(h, wr_ref[...], precision=lax.Precision.HIGHEST, preferred_element_type=F32)
    lane = lax.broadcasted_iota(jnp.int32, logits.shape, 1).astype(F32)
    neg = -jnp.inf
    l1 = jnp.where(lane < n_exp, logits, neg)
    m1 = jnp.max(l1, axis=-1, keepdims=True)
    i1 = jnp.min(jnp.where(l1 == m1, lane, float(LANES)), axis=-1, keepdims=True)
    l2 = jnp.where(lane == i1, neg, l1)
    m2 = jnp.max(l2, axis=-1, keepdims=True)
    i2 = jnp.min(jnp.where(l2 == m2, lane, float(LANES)), axis=-1, keepdims=True)
    e = jnp.exp(m2 - m1)
    p1 = 1.0 / (1.0 + e)
    p2 = e / (1.0 + e)
    gt_ref[0] = (jnp.where(lane == 0.0, i1, 0.0) + jnp.where(lane == 1.0, i2, 0.0)
                 + jnp.where(lane == 2.0, p1, 0.0) + jnp.where(lane == 3.0, p2, 0.0))


def _combine_norm_body(x_ref, y0_ref, y1_ref, gate_ref, g_ref, o_ref):
    x = x_ref[0] + gate_ref[0] * (y0_ref[...] + y1_ref[...])
    r = lax.rsqrt(jnp.mean(x * x, axis=-1, keepdims=True) + EPS)
    o_ref[0] = x * r * g_ref[...]


def _norm_mod(x3, g, sc, sh, w_router_pad=None, n_exp=0):
    b, s, d = x3.shape
    ts = min(256, s)
    row = pl.BlockSpec((1, ts, d), lambda bb, i: (bb, i, 0))
    vec = pl.BlockSpec((1, 1, d), lambda bb, i: (bb, 0, 0))
    in_specs = [row, pl.BlockSpec((1, d), lambda bb, i: (0, 0)), vec, vec]
    args = [x3, g.reshape(1, d), sc, sh]
    if w_router_pad is None:
        return pl.pallas_call(
            _norm_body, out_shape=jax.ShapeDtypeStruct((b, s, d), BF16), grid=(b, s // ts),
            in_specs=in_specs, out_specs=row,
            compiler_params=_params(("parallel", "parallel"), 40), name="norm_mod")(*args)
    in_specs.append(pl.BlockSpec((d, LANES), lambda bb, i: (0, 0)))
    args.append(w_router_pad)
    return pl.pallas_call(
        functools.partial(_norm_router_body, n_exp=n_exp),
        out_shape=(jax.ShapeDtypeStruct((b, s, d), F32), jax.ShapeDtypeStruct((b, s, LANES), F32)),
        grid=(b, s // ts), in_specs=in_specs,
        out_specs=(row, pl.BlockSpec((1, ts, LANES), lambda bb, i: (bb, i, 0))),
        compiler_params=_params(("parallel", "parallel"), 40), name="norm_mod_router")(*args)


def _combine_norm(x3, y2, gate, g):
    b, s, d = x3.shape
    ts = min(256, s)
    nb = s // ts
    row = pl.BlockSpec((1, ts, d), lambda bb, i: (bb, i, 0))
    return pl.pallas_call(
        _combine_norm_body, out_shape=jax.ShapeDtypeStruct((b, s, d), F32), grid=(b, nb),
        in_specs=[row,
                  pl.BlockSpec((ts, d), lambda bb, i: (bb * nb + i, 0)),
                  pl.BlockSpec((ts, d), lambda bb, i: ((b + bb) * nb + i, 0)),
                  pl.BlockSpec((1, 1, d), lambda bb, i: (bb, 0, 0)),
                  pl.BlockSpec((1, d), lambda bb, i: (0, 0))],
        out_specs=row,
        compiler_params=_params(("parallel", "parallel"), 48), name="moe_combine_norm")(x3, y2, y2, gate, g.reshape(1, d))


def _moe_dispatch(routing, n_exp, tile):
    m = routing.shape[0]
    a_tot = 2 * m
    p_tot = a_tot + n_exp * tile
    e_a = jnp.concatenate([routing[:, 0], routing[:, 1]]).astype(jnp.int32)
    p_a = jnp.concatenate([routing[:, 2], routing[:, 3]])
    experts = jnp.arange(n_exp, dtype=jnp.int32)
    cnt = jnp.sum((e_a[:, None] == experts[None, :]).astype(jnp.int32), axis=0)
    n_pad = (tile - cnt % tile) % tile
    pad_key = jnp.where(jnp.arange(tile, dtype=jnp.int32)[None, :] < n_pad[:, None], experts[:, None], n_exp)
    keys = jnp.concatenate([e_a, pad_key.reshape(-1)])
    ids = jnp.concatenate([jnp.arange(a_tot, dtype=jnp.int32), jnp.full((n_exp * tile,), -1, jnp.int32)])
    wts = jnp.concatenate([p_a, jnp.zeros((n_exp * tile,), F32)])
    keys, ids, wts = lax.sort((keys, ids, wts), num_keys=1, is_stable=True)
    valid = ids >= 0
    src_tok = jnp.where(valid, ids % m, 0)
    pad_ord = jnp.cumsum(jnp.logical_not(valid).astype(jnp.int32)) - 1
    dst_row = jnp.where(valid, ids, a_tot + pad_ord)
    tile_key = keys[::tile]
    tile_exp = jnp.minimum(tile_key, n_exp - 1)
    tile_valid = (tile_key < n_exp).astype(jnp.int32)
    return src_tok, dst_row, wts.reshape(p_tot, 1), tile_exp, tile_valid


def _for_rows(tile, fn):
    def body(r, c):
        fn(r)
        return c
    lax.fori_loop(0, tile, body, 0, unroll=8)


def _moe_up_body(src_ref, texp_ref, tval_ref, h_hbm, wg_ref, wu_ref, rg_ref, o_ref, x32_ref, xb_ref, sem,
                 *, tile, nt):
    t, jf = pl.program_id(0), pl.program_id(1)
    valid = tval_ref[t] == 1

    def row_copy(tt, r):
        slot = tt % 2
        tok = src_ref[tt * tile + r]
        return pltpu.make_async_copy(h_hbm.at[pl.ds(tok, 1), :], x32_ref.at[slot, pl.ds(r, 1), :], sem.at[slot])

    @pl.when(jnp.logical_and(jf == 0, t == 0))
    def _():
        _for_rows(tile, lambda r: row_copy(t, r).start())

    @pl.when(jf == 0)
    def _():
        @pl.when(t + 1 < nt)
        def _():
            _for_rows(tile, lambda r: row_copy(t + 1, r).start())

        _for_rows(tile, lambda r: row_copy(t, r).wait())
        xb_ref[...] = x32_ref[t % 2].astype(BF16)

    @pl.when(valid)
    def _():
        x = xb_ref[...]
        a = _dot(x, wg_ref[0])
        u = _dot(x, wu_ref[0])
        o_ref[...] = (_silu(a) * u * rg_ref[...]).astype(o_ref.dtype)

    @pl.when(jnp.logical_not(valid))
    def _():
        o_ref[...] = jnp.zeros_like(o_ref)


def _moe_down_body(dst_ref, texp_ref, tval_ref, a_ref, wd_ref, out_hbm, y_ref, sem, *, tile, nt):
    t = pl.program_id(0)
    slot = t % 2

    def row_copy(tt, r):
        s = tt % 2
        row = dst_ref[tt * tile + r]
        return pltpu.make_async_copy(y_ref.at[s, pl.ds(r, 1), :], out_hbm.at[pl.ds(row, 1), :], sem.at[s])

    @pl.when(tval_ref[t] == 1)
    def _():
        y_ref[slot] = _dot(a_ref[...], wd_ref[0])

    @pl.when(tval_ref[t] != 1)
    def _():
        y_ref[slot] = jnp.zeros(y_ref.shape[1:], F32)

    _for_rows(tile, lambda r: row_copy(t, r).start())

    @pl.when(t > 0)
    def _():
        _for_rows(tile, lambda r: row_copy(t - 1, r).wait())

    @pl.when(t == nt - 1)
    def _():
        _for_rows(tile, lambda r: row_copy(t, r).wait())


def _moe_experts(h32, routing, wg, wu, wd):
    m, d = h32.shape
    n_exp, _, f = wg.shape
    tile = min(MOE_TILE, m)
    src_tok, dst_row, row_gate, tile_exp, tile_valid = _moe_dispatch(routing, n_exp, tile)
    p_tot = src_tok.shape[0]
    nt = p_tot // tile
    tn = min(MM_TILE // 2, f)
    nf = f // tn
    a = pl.pallas_call(
        functools.partial(_moe_up_body, tile=tile, nt=nt),
        out_shape=jax.ShapeDtypeStruct((p_tot, f), BF16),
        grid_spec=pltpu.PrefetchScalarGridSpec(
            num_scalar_prefetch=3,
            grid=(nt, nf),
            in_specs=[pl.BlockSpec(memory_space=pl.ANY),
                      pl.BlockSpec((1, d, tn), lambda t, j, src, te, tv: (te[t], 0, j)),
                      pl.BlockSpec((1, d, tn), lambda t, j, src, te, tv: (te[t], 0, j)),
                      pl.BlockSpec((tile, 1), lambda t, j, src, te, tv: (t, 0))],
            out_specs=pl.BlockSpec((tile, tn), lambda t, j, src, te, tv: (t, j)),
            scratch_shapes=[pltpu.VMEM((2, tile, d), F32), pltpu.VMEM((tile, d), BF16),
                            pltpu.SemaphoreType.DMA((2,))]),
        compiler_params=_params(("arbitrary", "arbitrary"), 56),
        name="moe_up",
    )(src_tok, tile_exp, tile_valid, h32, wg, wu, row_gate)
    return pl.pallas_call(
        functools.partial(_moe_down_body, tile=tile, nt=nt),
        out_shape=jax.ShapeDtypeStruct((p_tot, d), F32),
        grid_spec=pltpu.PrefetchScalarGridSpec(
            num_scalar_prefetch=3,
            grid=(nt,),
            in_specs=[pl.BlockSpec((tile, f), lambda t, dst, te, tv: (t, 0)),
                      pl.BlockSpec((1, f, d), lambda t, dst, te, tv: (te[t], 0, 0))],
            out_specs=pl.BlockSpec(memory_space=pl.ANY),
            scratch_shapes=[pltpu.VMEM((2, tile, d), F32), pltpu.SemaphoreType.DMA((2,))]),
        compiler_params=_params(("arbitrary",), 56),
        name="moe_down",
    )(dst_row, tile_exp, tile_valid, a, wd)


def _dft_chan_body(u_ref, w_ref, o_ref):
    r = _dot(u_ref[0], w_ref[...])
    gd = r.shape[1] // 2
    o_ref[0, 0] = r[:, :gd].astype(o_ref.dtype)
    o_ref[0, 1] = r[:, gd:].astype(o_ref.dtype)


def _dft_outer_body(x_ref, f_ref, tc_ref, ts_ref, o_ref, *, rows):
    n2 = f_ref.shape[0] // 2
    for l in range(rows):
        pq = jnp.concatenate([x_ref[0, 0, :, l, :], x_ref[0, 1, :, l, :]], axis=0).astype(BF16)
        r = _dot(f_ref[...], pq)
        yr, yi = r[:n2], r[n2:]
        tc, ts = tc_ref[l], ts_ref[l]
        o_ref[0, 0, :, l, :] = yr * tc + yi * ts
        o_ref[0, 1, :, l, :] = yi * tc - yr * ts


def _dft_inner_body(y_ref, f_ref, o_ref, o32_ref, *, kb, scale):
    for kk in range(kb):
        rhs = jnp.concatenate([y_ref[0, 0, kk], y_ref[0, 1, kk]], axis=0).astype(BF16)
        o32_ref[:, kk, :] = _dot(f_ref[...], rhs) * scale
    o_ref[0] = o32_ref[...].astype(o_ref.dtype)


def _cos_sin(rows, cols, period):
    ang = ((rows[:, None] * cols[None, :]) % period).astype(F32) * (2.0 * math.pi / period)
    return jnp.cos(ang), jnp.sin(ang)


def _fourier_mix(proj3, a_width):
    b, s, _ = proj3.shape
    gd = a_width // A_GROUPS
    n2 = LANES
    n1 = s // n2
    i_gd, i1, i2 = (jnp.arange(n, dtype=jnp.int32) for n in (gd, n1, n2))
    cc, sc = _cos_sin(i_gd, i_gd, gd)
    w_chan = jnp.concatenate([cc, sc], axis=1).astype(BF16)
    c2, s2 = _cos_sin(i2, i2, n2)
    f_outer = jnp.concatenate([jnp.concatenate([c2, -s2], axis=1),
                               jnp.concatenate([-s2, -c2], axis=1)], axis=0).astype(BF16)
    tw_c, tw_s = _cos_sin(i1, i2, s)
    c1, s1 = _cos_sin(i1, i1, n1)
    f_inner = jnp.concatenate([c1, s1], axis=1).astype(BF16)
    ts = min(1024, s)
    z = pl.pallas_call(
        _dft_chan_body,
        out_shape=jax.ShapeDtypeStruct((b, 2, s, a_width), F32),
        grid=(b, s // ts, A_GROUPS),
        in_specs=[pl.BlockSpec((1, ts, gd), lambda bb, i, g: (bb, i, g)),
                  pl.BlockSpec((gd, 2 * gd), lambda bb, i, g: (0, 0))],
        out_specs=pl.BlockSpec((1, 2, ts, gd), lambda bb, i, g: (bb, 0, i, g)),
        compiler_params=_params(("parallel", "parallel", "parallel"), 40),
        name="dft_channels",
    )(proj3, w_chan)
    rows, tc = 8, min(512, a_width)
    y = pl.pallas_call(
        functools.partial(_dft_outer_body, rows=rows),
        out_shape=jax.ShapeDtypeStruct((b, 2, n2, n1, a_width), F32),
        grid=(b, n1 // rows, a_width // tc),
        in_specs=[pl.BlockSpec((1, 2, n2, rows, tc), lambda bb, j, c: (bb, 0, 0, j, c)),
                  pl.BlockSpec((2 * n2, 2 * n2), lambda bb, j, c: (0, 0)),
                  pl.BlockSpec((rows, n2, 1), lambda bb, j, c: (j, 0, 0)),
                  pl.BlockSpec((rows, n2, 1), lambda bb, j, c: (j, 0, 0))],
        out_specs=pl.BlockSpec((1, 2, n2, rows, tc), lambda bb, j, c: (bb, 0, 0, j, c)),
        compiler_params=_params(("parallel", "parallel", "parallel"), 40),
        name="dft_positions_outer",
    )(z.reshape(b, 2, n2, n1, a_width), f_outer, tw_c.reshape(n1, n2, 1), tw_s.reshape(n1, n2, 1))
    kb, tci = BF16_ROWS, min(1024, a_width)
    out = pl.pallas_call(
        functools.partial(_dft_inner_body, kb=kb, scale=1.0 / math.sqrt(s * gd)),
        out_shape=jax.ShapeDtypeStruct((b, n1, n2, a_width), BF16),
        grid=(b, n2 // kb, a_width // tci),
        in_specs=[pl.BlockSpec((1, 2, kb, n1, tci), lambda bb, k, c: (bb, 0, k, 0, c)),
                  pl.BlockSpec((n1, 2 * n1), lambda bb, k, c: (0, 0))],
        out_specs=pl.BlockSpec((1, n1, kb, tci), lambda bb, k, c: (bb, 0, k, c)),
        scratch_shapes=[pltpu.VMEM((n1, kb, tci), F32)],
        compiler_params=_params(("parallel", "parallel", "parallel"), 40),
        name="dft_positions_inner",
    )(y, f_inner)
    return out.reshape(b, s, a_width)


def _rotate(x, cos, sin, half):
    x1, x2 = x[:, :half], x[:, half:]
    return x1 * cos - x2 * sin, x1 * sin + x2 * cos


def _ret_state_body(lg_ref, k_ref, v_ref, cos_ref, sin_ref, sb_ref, s_ref, *, chunk, heads, half, qk_scale):
    h, c = pl.program_id(1), pl.program_id(2)

    @pl.when(c == 0)
    def _():
        s_ref[...] = jnp.zeros_like(s_ref)

    sb_ref[0, 0, 0] = s_ref[...].astype(sb_ref.dtype)
    lgb = lg_ref[heads + h]
    k1, k2 = _rotate(k_ref[0].astype(F32), cos_ref[...], sin_ref[...], half)
    j = lax.broadcasted_iota(jnp.int32, (chunk, 1), 0).astype(F32)
    wj = jnp.exp(j * lgb) * qk_scale
    kd_t = jnp.concatenate([k1 * wj, k2 * wj], axis=1).T.astype(BF16)
    carry = jnp.exp(jnp.full((1, s_ref.shape[1]), float(chunk), F32) * lgb)
    s_ref[...] = s_ref[...] * carry + _dot(kd_t, v_ref[0])


def _ret_main_body(lg_ref, q_ref, k_ref, v_ref, g_ref, cos_ref, sin_ref, sb_ref, y_ref, s_ref, d_ref,
                   *, chunk, heads, half, qk_scale):
    h, c = pl.program_id(1), pl.program_id(2)
    lgf, lgb = lg_ref[h], lg_ref[heads + h]

    @pl.when(c == 0)
    def _():
        s_ref[...] = jnp.zeros_like(s_ref)
        i = lax.broadcasted_iota(jnp.int32, (chunk, chunk), 0)
        j = lax.broadcasted_iota(jnp.int32, (chunk, chunk), 1)
        d = (i - j).astype(F32)
        d_ref[...] = jnp.exp(jnp.where(d >= 0.0, d * lgf, -d * lgb))

    cos, sin = cos_ref[...], sin_ref[...]
    q1, q2 = _rotate(q_ref[0].astype(F32), cos, sin, half)
    k1, k2 = _rotate(k_ref[0].astype(F32), cos, sin, half)
    k1, k2 = k1 * qk_scale, k2 * qk_scale
    v = v_ref[0]
    qr = jnp.concatenate([q1, q2], axis=1)
    kr = jnp.concatenate([k1, k2], axis=1)
    scores = _dot_nt(qr.astype(BF16), kr.astype(BF16))
    o = _dot((scores * d_ref[...]).astype(BF16), v)
    ii = lax.broadcasted_iota(jnp.int32, (chunk, 1), 0).astype(F32)
    ef = jnp.exp((ii + 1.0) * lgf)
    eb = jnp.exp((float(chunk) - ii) * lgb)
    qq = jnp.concatenate([qr * ef, qr * eb], axis=1).astype(BF16)
    st = jnp.concatenate([s_ref[...].astype(BF16), sb_ref[0, 0, 0]], axis=0)
    o = o + _dot(qq, st)
    wj = jnp.exp((float(chunk - 1) - ii) * lgf)
    kd_t = (kr * wj).T.astype(BF16)
    carry = jnp.exp(jnp.full((1, s_ref.shape[1]), float(chunk), F32) * lgf)
    s_ref[...] = s_ref[...] * carry + _dot(kd_t, v)
    r = lax.rsqrt(jnp.mean(o * o, axis=-1, keepdims=True) + EPS)
    y_ref[0] = (o * r * _silu(g_ref[0].astype(F32))).astype(y_ref.dtype)


def _retention_mix(proj3, a_width, qk, vd):
    b, s, _ = proj3.shape
    heads, half = B_HEADS, qk // 2
    chunk = min(RET_CHUNK, s)
    nc = s // chunk
    q0 = a_width // qk
    k0 = q0 + heads
    v0 = (a_width + 2 * heads * qk) // vd
    g0 = v0 + heads
    inv = 1.0 / (ROPE_BASE ** jnp.linspace(0.0, 1.0, half, dtype=F32))
    ang = jnp.arange(s, dtype=F32)[:, None] * inv[None, :]
    cos, sin = jnp.cos(ang), jnp.sin(ang)
    hh = jnp.arange(heads, dtype=F32)
    lg = jnp.concatenate([jnp.log1p(-jnp.exp2(-5.0 - hh)), jnp.log1p(-jnp.exp2(-5.5 - hh))])
    qk_scale = float(qk) ** -0.5
    smem = pl.BlockSpec(memory_space=pltpu.SMEM)

    def rev(c):
        return nc - 1 - c

    sb = pl.pallas_call(
        functools.partial(_ret_state_body, chunk=chunk, heads=heads, half=half, qk_scale=qk_scale),
        out_shape=jax.ShapeDtypeStruct((b, heads, nc, qk, vd), BF16),
        grid=(b, heads, nc),
        in_specs=[smem,
                  pl.BlockSpec((1, chunk, qk), lambda bb, h, c: (bb, rev(c), k0 + h)),
                  pl.BlockSpec((1, chunk, vd), lambda bb, h, c: (bb, rev(c), v0 + h)),
                  pl.BlockSpec((chunk, half), lambda bb, h, c: (rev(c), 0)),
                  pl.BlockSpec((chunk, half), lambda bb, h, c: (rev(c), 0))],
        out_specs=pl.BlockSpec((1, 1, 1, qk, vd), lambda bb, h, c: (bb, h, rev(c), 0, 0)),
        scratch_shapes=[pltpu.VMEM((qk, vd), F32)],
        compiler_params=_params(("parallel", "parallel", "arbitrary"), 40),
        name="retention_bwd_states",
    )(lg, proj3, proj3, cos, sin)
    return pl.pallas_call(
        functools.partial(_ret_main_body, chunk=chunk, heads=heads, half=half, qk_scale=qk_scale),
        out_shape=jax.ShapeDtypeStruct((b, s, heads * vd), BF16),
        grid=(b, heads, nc),
        in_specs=[smem,
                  pl.BlockSpec((1, chunk, qk), lambda bb, h, c: (bb, c, q0 + h)),
                  pl.BlockSpec((1, chunk, qk), lambda bb, h, c: (bb, c, k0 + h)),
                  pl.BlockSpec((1, chunk, vd), lambda bb, h, c: (bb, c, v0 + h)),
                  pl.BlockSpec((1, chunk, vd), lambda bb, h, c: (bb, c, g0 + h)),
                  pl.BlockSpec((chunk, half), lambda bb, h, c: (c, 0)),
                  pl.BlockSpec((chunk, half), lambda bb, h, c: (c, 0)),
                  pl.BlockSpec((1, 1, 1, qk, vd), lambda bb, h, c: (bb, h, c, 0, 0))],
        out_specs=pl.BlockSpec((1, chunk, vd), lambda bb, h, c: (bb, c, h)),
        scratch_shapes=[pltpu.VMEM((qk, vd), F32), pltpu.VMEM((chunk, chunk), F32)],
        compiler_params=_params(("parallel", "parallel", "arbitrary"), 40),
        name="retention_main",
    )(lg, proj3, proj3, proj3, proj3, cos, sin, sb)


def _conv_rows(ext_ref, cur_ref, prev_ref, next_ref, cw_ref, cb_ref, blk, nblk, ts):
    hr = BF16_ROWS
    prev = prev_ref[0].astype(F32)
    nxt = next_ref[0].astype(F32)
    ext_ref[0:hr, :] = jnp.where(blk > 0, prev, 0.0)
    ext_ref[hr:hr + ts, :] = cur_ref[0].astype(F32)
    ext_ref[hr + ts:hr + ts + hr, :] = jnp.where(blk < nblk - 1, nxt, 0.0)
    acc = cb_ref[...] + cw_ref[0:1, :] * ext_ref[pl.ds(hr - 2, ts), :]
    for t in range(1, CONV_K):
        acc = acc + cw_ref[t:t + 1, :] * ext_ref[pl.ds(hr - 2 + t, ts), :]
    return acc


def _conv_silu_body(cur_ref, prev_ref, next_ref, cw_ref, cb_ref, o_ref, ext_ref, *, ts, nblk):
    blk = pl.program_id(1)
    xc = _conv_rows(ext_ref, cur_ref, prev_ref, next_ref, cw_ref, cb_ref, blk, nblk, ts)
    o_ref[0] = _silu(xc).astype(o_ref.dtype)


def _conv_silu(proj3, col0, width, cw, cb):
    b, s, _ = proj3.shape
    ts, tc = min(512, s), 1024
    nblk, per, last = s // ts, ts // BF16_ROWS, s // BF16_ROWS - 1
    c0 = col0 // tc
    return pl.pallas_call(
        functools.partial(_conv_silu_body, ts=ts, nblk=nblk),
        out_shape=jax.ShapeDtypeStruct((b, s, width), BF16),
        grid=(b, nblk, width // tc),
        in_specs=[pl.BlockSpec((1, ts, tc), lambda bb, i, j: (bb, i, c0 + j)),
                  pl.BlockSpec((1, BF16_ROWS, tc), lambda bb, i, j: (bb, jnp.maximum(i * per - 1, 0), c0 + j)),
                  pl.BlockSpec((1, BF16_ROWS, tc), lambda bb, i, j: (bb, jnp.minimum((i + 1) * per, last), c0 + j)),
                  pl.BlockSpec((CONV_K, tc), lambda bb, i, j: (0, j)),
                  pl.BlockSpec((1, tc), lambda bb, i, j: (0, j))],
        out_specs=pl.BlockSpec((1, ts, tc), lambda bb, i, j: (bb, i, j)),
        scratch_shapes=[pltpu.VMEM((ts + 2 * BF16_ROWS, tc), F32)],
        compiler_params=_params(("parallel", "parallel", "parallel"), 40),
        name="ssd_conv_silu",
    )(proj3, proj3, proj3, cw, cb.reshape(1, width))


def _lru_body(*refs, ts, nblk, reverse, bdim):
    if reverse:
        (cur_ref, prev_ref, next_ref, cw_ref, cb_ref, wa_ref, ba_ref, wx_ref, bx_ref, lam_ref,
         hf_ref, gc_ref, o_ref, ext_ref, a_ref, b_ref, h_ref) = refs
    else:
        (cur_ref, prev_ref, next_ref, cw_ref, cb_ref, wa_ref, ba_ref, wx_ref, bx_ref, lam_ref,
         o_ref, ext_ref, a_ref, b_ref, h_ref) = refs
    i = pl.program_id(1)
    blk = nblk - 1 - i if reverse else i

    @pl.when(i == 0)
    def _():
        h_ref[...] = jnp.zeros_like(h_ref)

    xc = _conv_rows(ext_ref, cur_ref, prev_ref, next_ref, cw_ref, cb_ref, blk, nblk, ts)
    rate = -LRU_C * _softplus(-lam_ref[0])
    for n in range(C_BLOCKS):
        cols = slice(n * bdim, (n + 1) * bdim)
        xn = xc[:, cols]
        xb = xn.astype(BF16)
        r = _sigmoid(_dot(xb, wa_ref[0, n]) + ba_ref[0][:, cols])
        ig = _sigmoid(_dot(xb, wx_ref[0, n]) + bx_ref[0][:, cols])
        a = jnp.exp(rate[:, cols] * r)
        a_ref[:, cols] = a
        b_ref[:, cols] = jnp.sqrt(1.0 - a * a) * (ig * xn)

    def step(t, h):
        tt = ts - 1 - t if reverse else t
        h = a_ref[pl.ds(tt, 1), :] * h + b_ref[pl.ds(tt, 1), :]
        b_ref[pl.ds(tt, 1), :] = h
        return h

    h_ref[...] = lax.fori_loop(0, ts, step, h_ref[...], unroll=8)
    if reverse:
        o_ref[0] = ((hf_ref[0] + b_ref[...]) * _gelu_tanh(gc_ref[0].astype(F32))).astype(o_ref.dtype)
    else:
        o_ref[0] = b_ref[...]


def _lru_pass(proj3, cw, cb, wa, ba, wx, bx, lam, direction, hf=None):
    b, s, _ = proj3.shape
    cwid = cw.shape[1]
    bdim = cwid // C_BLOCKS
    ts = min(256, s)
    nblk, per, last = s // ts, ts // BF16_ROWS, s // BF16_ROWS - 1
    reverse = direction == 1

    def blk(i):
        return nblk - 1 - i if reverse else i

    vec = pl.BlockSpec((1, 1, cwid), lambda bb, i: (direction, 0, 0))
    wblk = pl.BlockSpec((1, C_BLOCKS, bdim, bdim), lambda bb, i: (direction, 0, 0, 0))
    in_specs = [pl.BlockSpec((1, ts, cwid), lambda bb, i: (bb, blk(i), 1)),
                pl.BlockSpec((1, BF16_ROWS, cwid), lambda bb, i: (bb, jnp.maximum(blk(i) * per - 1, 0), 1)),
                pl.BlockSpec((1, BF16_ROWS, cwid), lambda bb, i: (bb, jnp.minimum((blk(i) + 1) * per, last), 1)),
                pl.BlockSpec((CONV_K, cwid), lambda bb, i: (0, 0)),
                pl.BlockSpec((1, cwid), lambda bb, i: (0, 0)),
                wblk, vec, wblk, vec, vec]
    args = [proj3, proj3, proj3, cw, cb.reshape(1, cwid), wa, ba.reshape(2, 1, cwid), wx,
            bx.reshape(2, 1, cwid), lam.reshape(2, 1, cwid)]
    row = pl.BlockSpec((1, ts, cwid), lambda bb, i: (bb, blk(i), 0))
    if reverse:
        in_specs += [row, row]
        args += [hf, proj3]
        out_dtype = BF16
    else:
        out_dtype = F32
    return pl.pallas_call(
        functools.partial(_lru_body, ts=ts, nblk=nblk, reverse=reverse, bdim=bdim),
        out_shape=jax.ShapeDtypeStruct((b, s, cwid), out_dtype),
        grid=(b, nblk),
        in_specs=in_specs,
        out_specs=row,
        scratch_shapes=[pltpu.VMEM((ts + 2 * BF16_ROWS, cwid), F32), pltpu.VMEM((ts, cwid), F32),
                        pltpu.VMEM((ts, cwid), F32), pltpu.VMEM((1, cwid), F32)],
        compiler_params=_params(("parallel", "arbitrary"), 48),
        name="rglru_bwd" if reverse else "rglru_fwd",
    )(*args)


def _cumsum_rows(x):
    n = x.shape[0]
    tri = (lax.broadcasted_iota(jnp.int32, (n, n), 0) >= lax.broadcasted_iota(jnp.int32, (n, n), 1))
    tri = jnp.where(tri, 1.0, 0.0).astype(BF16)
    hi = x.astype(BF16)
    r1 = x - hi.astype(F32)
    mid = r1.astype(BF16)
    lo = (r1 - mid.astype(F32)).astype(BF16)
    return _dot(tri, hi) + _dot(tri, mid) + _dot(tri, lo)


def _ssd_decays(dt_raw, bias, alog):
    dt = _softplus(dt_raw + bias)
    la = -dt * jnp.exp(alog)
    cs = _cumsum_rows(la)
    return dt, la, cs


def _expand_heads(x, expand):
    hi = x.astype(BF16)
    lo = (x - hi.astype(F32)).astype(BF16)
    return _dot(jnp.concatenate([hi, lo], axis=1), expand)


def _ssd_state_update(s, bm, xs32, w_exp, carry):
    bm_t = bm.astype(F32).T.astype(BF16)
    return s * carry + _dot(bm_t, (xs32 * w_exp).astype(BF16))


def _ssd_state_body(xs_ref, bm_ref, dt_ref, bias_ref, alog_ref, expb_ref, sbo_ref, s_ref, *, chunk, gps):
    c, gi = pl.program_id(1), pl.program_id(2)
    gw = D_HPG * D_HEADDIM
    for gg in range(gps):
        g = gi * gps + gg

        @pl.when(c == 0)
        def _():
            s_ref[g] = jnp.zeros(s_ref.shape[1:], F32)

        s = s_ref[g]
        sbo_ref[0, 0, gg] = s.astype(sbo_ref.dtype)
        dt, la, cs = _ssd_decays(dt_ref[0, :, gg * LANES:(gg + 1) * LANES], bias_ref[gg], alog_ref[gg])
        w_exp = _expand_heads(jnp.exp(cs - la) * dt, expb_ref[...])
        carry = _expand_heads(jnp.exp(cs[chunk - 8:chunk, :]), expb_ref[...])[7:8, :]
        s_ref[g] = _ssd_state_update(s, bm_ref[0, :, gg * D_STATE:(gg + 1) * D_STATE],
                                     xs_ref[0, :, gg * gw:(gg + 1) * gw].astype(F32), w_exp, carry)


def _ssd_group(xs, bm, cm, z, dt_raw, bias, alog, dsk, nw, s, sb_bf, expf, expb, chunk, causal, lane):
    dt, la, cs = _ssd_decays(dt_raw, bias, alog)
    cbx = cs - la
    tot = cs[chunk - 1:chunk, :]
    e_f = _expand_heads(jnp.exp(cs), expf)
    e_b = _expand_heads(jnp.exp(tot - cbx), expb)
    w_f = _expand_heads(jnp.exp(tot - cs) * dt, expf)
    comb_t = jnp.where(lane < D_HPG, cs, jnp.where(lane < 2 * D_HPG, cbx, dt)).T
    scores = _dot_nt(cm, bm)
    xs32 = xs.astype(F32)
    y_f = _dot(cm, s.astype(BF16))
    y_b = _dot(cm, sb_bf)
    zeros = jnp.zeros((chunk, LANES), BF16)
    lane_lo = lane < D_HEADDIM
    ys = []
    for p in range(D_HPG // 2):
        xs_p = xs[:, p * LANES:(p + 1) * LANES]
        ws = []
        for q in range(2):
            hf = 2 * p + q
            hb = D_HPG + hf
            arg = jnp.where(causal, cs[:, hf:hf + 1] - comb_t[hf:hf + 1, :], comb_t[hb:hb + 1, :] - cbx[:, hb:hb + 1])
            dsel = jnp.where(causal, comb_t[2 * D_HPG + hf:2 * D_HPG + hf + 1, :],
                             comb_t[2 * D_HPG + hb:2 * D_HPG + hb + 1, :])
            ws.append((scores * jnp.exp(arg) * dsel).astype(BF16))
        rhs = jnp.concatenate([jnp.where(lane_lo, xs_p, zeros), jnp.where(lane_lo, zeros, xs_p)], axis=0)
        ys.append(_dot(jnp.concatenate(ws, axis=1), rhs))
    y = jnp.concatenate(ys, axis=1) + y_f * e_f + y_b * e_b
    s_new = _ssd_state_update(s, bm, xs32, w_f, e_f[chunk - 1:chunk, :])
    y = (y + xs32 * dsk) * _silu(z.astype(F32))
    r = lax.rsqrt(jnp.mean(y * y, axis=-1, keepdims=True) + EPS)
    return (y * r * nw).astype(BF16), s_new


def _ssd_main_body(xs_ref, bm_ref, cm_ref, z_ref, dt_ref, bias_ref, alog_ref, dsk_ref, nw_ref, sb_ref,
                   expf_ref, expb_ref, y_ref, s_ref, *, chunk, gps):
    c, gi = pl.program_id(1), pl.program_id(2)
    gw = D_HPG * D_HEADDIM
    causal = (lax.broadcasted_iota(jnp.int32, (chunk, chunk), 0)
              >= lax.broadcasted_iota(jnp.int32, (chunk, chunk), 1))
    lane = lax.broadcasted_iota(jnp.int32, (chunk, LANES), 1)
    for gg in range(gps):
        g = gi * gps + gg

        @pl.when(c == 0)
        def _():
            s_ref[g] = jnp.zeros(s_ref.shape[1:], F32)

        wide = slice(gg * gw, (gg + 1) * gw)
        st = slice(gg * D_STATE, (gg + 1) * D_STATE)
        y, s_new = _ssd_group(xs_ref[0, :, wide], bm_ref[0, :, st], cm_ref[0, :, st], z_ref[0, :, wide],
                              dt_ref[0, :, gg * LANES:(gg + 1) * LANES], bias_ref[gg], alog_ref[gg],
                              dsk_ref[gg], nw_ref[gg], s_ref[g], sb_ref[0, 0, gg], expf_ref[...], expb_ref[...],
                              chunk, causal, lane)
        s_ref[g] = s_new
        y_ref[0, :, wide] = y


def _ssd_mix(proj3, z_col0, xbc_act, dt_pad, bias_pad, alog_pad, d_skip, norm_w):
    b, s, _ = xbc_act.shape
    gw = D_HPG * D_HEADDIM
    inner = D_GROUPS * gw
    chunk = min(SSD_CHUNK, s)
    nc = s // chunk
    gps = SSD_GROUPS_PER_STEP
    ng = D_GROUPS // gps
    b0 = inner // (gps * D_STATE)
    c0 = b0 + ng
    z0 = z_col0 // (gps * gw)

    def rev(c):
        return nc - 1 - c

    def expand_matrix(first):
        k = jnp.arange(2 * LANES, dtype=jnp.int32)[:, None] % LANES
        head = jnp.arange(gw, dtype=jnp.int32)[None, :] // D_HEADDIM
        return jnp.where(k == first + head, 1.0, 0.0).astype(BF16)

    expf, expb = expand_matrix(0), expand_matrix(D_HPG)
    emat = pl.BlockSpec((2 * LANES, gw), lambda bb, c, g: (0, 0))
    gvec = pl.BlockSpec((gps, 1, LANES), lambda bb, c, g: (g, 0, 0))
    sb = pl.pallas_call(
        functools.partial(_ssd_state_body, chunk=chunk, gps=gps),
        out_shape=jax.ShapeDtypeStruct((b, nc, D_GROUPS, D_STATE, gw), BF16),
        grid=(b, nc, ng),
        in_specs=[pl.BlockSpec((1, chunk, gps * gw), lambda bb, c, g: (bb, rev(c), g)),
                  pl.BlockSpec((1, chunk, gps * D_STATE), lambda bb, c, g: (bb, rev(c), b0 + g)),
                  pl.BlockSpec((1, chunk, gps * LANES), lambda bb, c, g: (bb, rev(c), g)),
                  gvec, gvec, emat],
        out_specs=pl.BlockSpec((1, 1, gps, D_STATE, gw), lambda bb, c, g: (bb, rev(c), g, 0, 0)),
        scratch_shapes=[pltpu.VMEM((D_GROUPS, D_STATE, gw), F32)],
        compiler_params=_params(("parallel", "arbitrary", "arbitrary"), 40),
        name="ssd_bwd_states",
    )(xbc_act, xbc_act, dt_pad, bias_pad, alog_pad, expb)
    gwide = pl.BlockSpec((gps, 1, gw), lambda bb, c, g: (g, 0, 0))
    return pl.pallas_call(
        functools.partial(_ssd_main_body, chunk=chunk, gps=gps),
        out_shape=jax.ShapeDtypeStruct((b, s, inner), BF16),
        grid=(b, nc, ng),
        in_specs=[pl.BlockSpec((1, chunk, gps * gw), lambda bb, c, g: (bb, c, g)),
                  pl.BlockSpec((1, chunk, gps * D_STATE), lambda bb, c, g: (bb, c, b0 + g)),
                  pl.BlockSpec((1, chunk, gps * D_STATE), lambda bb, c, g: (bb, c, c0 + g)),
                  pl.BlockSpec((1, chunk, gps * gw), lambda bb, c, g: (bb, c, z0 + g)),
                  pl.BlockSpec((1, chunk, gps * LANES), lambda bb, c, g: (bb, c, g)),
                  gvec, gvec, gwide, gwide,
                  pl.BlockSpec((1, 1, gps, D_STATE, gw), lambda bb, c, g: (bb, c, g, 0, 0)),
                  emat, emat],
        out_specs=pl.BlockSpec((1, chunk, gps * gw), lambda bb, c, g: (bb, c, g)),
        scratch_shapes=[pltpu.VMEM((D_GROUPS, D_STATE, gw), F32)],
        compiler_params=_params(("parallel", "arbitrary", "arbitrary"), 40),
        name="ssd_main",
    )(xbc_act, xbc_act, xbc_act, proj3, dt_pad, bias_pad, alog_pad,
      d_skip.reshape(D_GROUPS, 1, gw), norm_w.reshape(D_GROUPS, 1, gw), sb, expf, expb)


def _pad_heads(fwd, bwd):
    lead = fwd.shape[:-1]
    f = fwd.reshape(lead + (D_GROUPS, D_HPG))
    bk = bwd.reshape(lead + (D_GROUPS, D_HPG))
    pad = jnp.zeros(lead + (D_GROUPS, LANES - 4 * D_HPG), fwd.dtype)
    return jnp.concatenate([f, bk, f, bk, pad], axis=-1).reshape(lead + (D_GROUPS * LANES,))


def _mod_vectors(c_groups, w_mod, b_mod):
    d = c_groups[0].shape[1]
    c_all = jnp.concatenate(c_groups, axis=0)
    rows = c_all.shape[0]
    c_pad = jnp.zeros((-(-rows // 8) * 8, d), F32).at[:rows].set(c_all)
    m = _ada_mod(c_pad, w_mod, b_mod)
    out, lo = [], 0
    for c in c_groups:
        mg = m[lo:lo + c.shape[0]]
        out.append([mg[:, i * d:(i + 1) * d].reshape(c.shape[0], 1, d) for i in range(6)])
        lo += c.shape[0]
    return out


def _even_layer(x3, mods, p):
    b, s, d = x3.shape
    sh1, sc1, gt1, sh2, sc2, gt2 = mods
    m = b * s
    h = _norm_mod(x3, p["g_mix_e"], sc1, sh1).reshape(m, d)
    proj = _matmul(h, p["w_in_e"], out_dtype=BF16, name="in_proj_even")
    proj3 = proj.reshape(b, s, -1)
    a_width = d // 2
    qk = d // 16
    y_a = _fourier_mix(proj3, a_width)
    y_b = _retention_mix(proj3, a_width, qk, 2 * qk)
    x2 = _matmul(y_a.reshape(m, -1), p["w_out_e"], x2=y_b.reshape(m, -1), res=x3.reshape(m, d), gate=gt1,
                 rows_per_gate=s, out_dtype=F32, name="out_proj_even")
    h = _norm_mod(x2.reshape(b, s, d), p["g_ffn_e"], sc2, sh2).reshape(m, d)
    a = _glu_up(h, p["w_gate_e"], p["w_up_e"], None, name="ffn_up")
    x2 = _matmul(a, p["w_down_e"], res=x2, gate=gt2, rows_per_gate=s, out_dtype=F32, name="ffn_down")
    return x2.reshape(b, s, d)


def _odd_layer(x3, mods, p):
    b, s, d = x3.shape
    sh1, sc1, gt1, sh2, sc2, gt2 = mods
    m = b * s
    h = _norm_mod(x3, p["g_mix_o"], sc1, sh1).reshape(m, d)
    proj3 = _matmul(h, p["w_in_o_main"], out_dtype=BF16, name="in_proj_odd").reshape(b, s, -1)
    dt_pad = _matmul(h, p["w_in_o_dt"], out_dtype=F32, name="in_proj_dt").reshape(b, s, -1)
    cwid = d // 2
    hf = _lru_pass(proj3, p["conv_w_c"], p["conv_b_c"], p["lru_wa"], p["lru_ba"], p["lru_wx"], p["lru_bx"],
                   p["lru_lam"], 0)
    y_c = _lru_pass(proj3, p["conv_w_c"], p["conv_b_c"], p["lru_wa"], p["lru_ba"], p["lru_wx"], p["lru_bx"],
                    p["lru_lam"], 1, hf=hf)
    xbc0 = 2 * cwid + d
    xbc_act = _conv_silu(proj3, xbc0, p["conv_w_d"].shape[1], p["conv_w_d"], p["conv_b_d"])
    y_d = _ssd_mix(proj3, 2 * cwid, xbc_act, dt_pad, p["dt_bias_pad"], p["a_log_pad"], p["d_skip"],
                   p["ssd_norm_w"])
    x2 = _matmul(y_c.reshape(m, -1), p["w_out_o"], x2=y_d.reshape(m, -1), res=x3.reshape(m, d), gate=gt1,
                 rows_per_gate=s, out_dtype=F32, name="out_proj_odd")
    n_exp = p["w_gate_x"].shape[0]
    h32, routing = _norm_mod(x2.reshape(b, s, d), p["g_ffn_o"], sc2, sh2, p["w_router_pad"], n_exp)
    y2 = _moe_experts(h32.reshape(m, d), routing.reshape(m, LANES), p["w_gate_x"], p["w_up_x"], p["w_down_x"])
    return _combine_norm(x2.reshape(b, s, d), y2, gt2, p["g_final"])


def _trunk(x3, mods_e, mods_o, p):
    x3 = _even_layer(x3, mods_e, p)
    return _odd_layer(x3, mods_o, p)


def kernel(x_prompt, x_sample, c_prompt, c_sample, w_mod_e, b_mod_e, g_mix_e, g_ffn_e, w_in_e, w_out_e, w_gate_e, w_up_e, w_down_e, w_mod_o, b_mod_o, g_mix_o, g_ffn_o, w_in_o, conv_w_c, conv_b_c, lru_wa, lru_ba, lru_wx, lru_bx, lru_lam, conv_w_d, conv_b_d, ssd_a_log, ssd_dt_bias, ssd_d, ssd_norm_w, w_out_o, w_router, w_gate_x, w_up_x, w_down_x, g_final):
    d = x_prompt.shape[-1]
    n_main = w_in_o.shape[2] - 2 * D_GROUPS * D_HPG
    heads = D_GROUPS * D_HPG
    w_in_o0 = w_in_o[0]
    n_exp = w_router.shape[2]
    p = dict(
        w_mod_e=w_mod_e[0], b_mod_e=b_mod_e[0], g_mix_e=g_mix_e[0], g_ffn_e=g_ffn_e[0],
        w_in_e=w_in_e[0].astype(BF16), w_out_e=w_out_e[0].astype(BF16),
        w_gate_e=w_gate_e.astype(BF16), w_up_e=w_up_e.astype(BF16), w_down_e=w_down_e[0].astype(BF16),
        w_mod_o=w_mod_o[0], b_mod_o=b_mod_o[0], g_mix_o=g_mix_o[0], g_ffn_o=g_ffn_o[0],
        w_in_o_main=w_in_o0[:, :n_main].astype(BF16),
        w_in_o_dt=_pad_heads(w_in_o0[:, n_main:n_main + heads], w_in_o0[:, n_main + heads:]).astype(BF16),
        conv_w_c=conv_w_c[0], conv_b_c=conv_b_c[0],
        lru_wa=lru_wa[0].astype(BF16), lru_ba=lru_ba[0], lru_wx=lru_wx[0].astype(BF16), lru_bx=lru_bx[0],
        lru_lam=lru_lam[0],
        conv_w_d=conv_w_d[0], conv_b_d=conv_b_d[0],
        dt_bias_pad=_pad_heads(ssd_dt_bias[0, 0], ssd_dt_bias[0, 1]).reshape(D_GROUPS, 1, LANES),
        a_log_pad=_pad_heads(ssd_a_log[0, 0], ssd_a_log[0, 1]).reshape(D_GROUPS, 1, LANES),
        d_skip=jnp.repeat(ssd_d[0], D_HEADDIM), ssd_norm_w=ssd_norm_w[0],
        w_out_o=w_out_o[0].astype(BF16),
        w_router_pad=jnp.zeros((d, LANES), F32).at[:, :n_exp].set(w_router[0]),
        w_gate_x=w_gate_x[0].astype(BF16), w_up_x=w_up_x[0].astype(BF16),
        w_down_x=w_down_x[0].astype(BF16),
        g_final=g_final,
    )
    mods_e = _mod_vectors([c_prompt, c_sample], p["w_mod_e"], p["b_mod_e"])
    mods_o = _mod_vectors([c_prompt, c_sample], p["w_mod_o"], p["b_mod_o"])
    return (_trunk(x_prompt, mods_e[0], mods_o[0], p), _trunk(x_sample, mods_e[1], mods_o[1], p))
```

```python
import functools
import math

import jax
import jax.numpy as jnp
from jax import lax
from jax.experimental import pallas as pl
from jax.experimental.pallas import tpu as pltpu

F32 = jnp.float32
BF16 = jnp.bfloat16

EPS = 1e-6
ROPE_BASE = 10000.0
LRU_C = 8.0
CONV_K = 4
A_GROUPS = 4
B_HEADS = 8
C_BLOCKS = 8
D_HEADDIM = 64
D_STATE = 128
D_GROUPS = 8
D_HPG = 8
TOP_K = 2

LANES = 128
BF16_ROWS = 16
SSD_CHUNK = 128
SSD_GROUPS_PER_STEP = 4
RET_CHUNK = 512
MOE_TILE = 512
MM_TILE = 1024
MM_TK_MAX = 4096


def _params(sem, vmem_mb):
    return pltpu.CompilerParams(dimension_semantics=sem, vmem_limit_bytes=vmem_mb << 20)


def _dot(a, b):
    return jnp.dot(a, b, preferred_element_type=F32)


def _dot_nt(a, b):
    return lax.dot_general(a, b, (((1,), (1,)), ((), ())), preferred_element_type=F32)


def _sigmoid(x):
    return 1.0 / (1.0 + jnp.exp(-x))


def _silu(x):
    return x / (1.0 + jnp.exp(-x))


def _softplus(x):
    return jnp.maximum(x, 0.0) + jnp.log(1.0 + jnp.exp(-jnp.abs(x)))


def _gelu_tanh(x):
    return 0.5 * x * (1.0 + jnp.tanh(math.sqrt(2.0 / math.pi) * (x + 0.044715 * (x * x * x))))


def _mm_body(*refs, nk, nk1, has_x2, resid):
    refs = list(refs)
    x_ref = refs.pop(0)
    x2_ref = refs.pop(0) if has_x2 else None
    w_ref = refs.pop(0)
    res_ref = refs.pop(0) if resid else None
    gate_ref = refs.pop(0) if resid else None
    o_ref = refs.pop(0)
    acc_ref = refs.pop(0) if nk > 1 else None

    def finish(acc):
        if resid:
            o_ref[...] = res_ref[...] + gate_ref[0] * acc
        else:
            o_ref[...] = acc.astype(o_ref.dtype)

    if nk == 1:
        finish(_dot(x_ref[...], w_ref[...]))
        return

    k = pl.program_id(2)

    @pl.when(k == 0)
    def _():
        acc_ref[...] = _dot(x_ref[...], w_ref[...])

    if has_x2:
        if nk1 > 1:
            @pl.when(jnp.logical_and(k > 0, k < nk1))
            def _():
                acc_ref[...] += _dot(x_ref[...], w_ref[...])

        @pl.when(k >= nk1)
        def _():
            acc_ref[...] += _dot(x2_ref[...], w_ref[...])
    else:
        @pl.when(k > 0)
        def _():
            acc_ref[...] += _dot(x_ref[...], w_ref[...])

    @pl.when(k == nk - 1)
    def _():
        finish(acc_ref[...])


def _matmul(x, w, *, out_dtype, name, x2=None, res=None, gate=None, rows_per_gate=None, tk=None):
    m, k1 = x.shape
    k, n = w.shape
    tm, tn = min(MM_TILE, m), min(MM_TILE, n)
    if tk is None:
        tk = k if k <= MM_TK_MAX else MM_TK_MAX // 2
    if x2 is not None:
        tk = min(tk, math.gcd(k1, x2.shape[1]))
    nk, nk1 = k // tk, k1 // tk
    resid = res is not None
    assert not resid or rows_per_gate % tm == 0, "a row tile must not straddle two gate rows"
    in_specs = [pl.BlockSpec((tm, tk), lambda i, j, kk: (i, jnp.minimum(kk, nk1 - 1)))]
    args = [x]
    if x2 is not None:
        in_specs.append(pl.BlockSpec((tm, tk), lambda i, j, kk: (i, jnp.maximum(kk - nk1, 0))))
        args.append(x2)
    in_specs.append(pl.BlockSpec((tk, tn), lambda i, j, kk: (kk, j)))
    args.append(w)
    if resid:
        in_specs.append(pl.BlockSpec((tm, tn), lambda i, j, kk: (i, j)))
        in_specs.append(pl.BlockSpec((1, 1, tn), lambda i, j, kk: ((i * tm) // rows_per_gate, 0, j)))
        args += [res, gate]
    scratch = [pltpu.VMEM((tm, tn), F32)] if nk > 1 else []
    return pl.pallas_call(
        functools.partial(_mm_body, nk=nk, nk1=nk1, has_x2=x2 is not None, resid=resid),
        out_shape=jax.ShapeDtypeStruct((m, n), out_dtype),
        grid=(m // tm, n // tn, nk),
        in_specs=in_specs,
        out_specs=pl.BlockSpec((tm, tn), lambda i, j, kk: (i, j)),
        scratch_shapes=scratch,
        compiler_params=_params(("parallel", "parallel", "arbitrary"), 56),
        name=name,
    )(*args)


def _glu_body(*refs, has_gate):
    if has_gate:
        x_ref, wg_ref, wu_ref, gcol_ref, o_ref = refs
    else:
        x_ref, wg_ref, wu_ref, o_ref = refs
    x = x_ref[...]
    a = _dot(x, wg_ref[0])
    u = _dot(x, wu_ref[0])
    h = _silu(a) * u
    if has_gate:
        h = h * gcol_ref[0]
    o_ref[...] = h.astype(o_ref.dtype)


def _glu_up(x, wg, wu, gates_t, *, name):
    m, k = x.shape
    e, _, f = wg.shape
    tm, tn = min(MM_TILE, m), min(MM_TILE // 2, f)
    per = f // tn
    w_spec = pl.BlockSpec((1, k, tn), lambda i, j: (j // per, 0, j % per))
    in_specs = [pl.BlockSpec((tm, k), lambda i, j: (i, 0)), w_spec, w_spec]
    args = [x, wg, wu]
    if gates_t is not None:
        in_specs.append(pl.BlockSpec((1, tm, 1), lambda i, j: (j // per, i, 0)))
        args.append(gates_t)
    return pl.pallas_call(
        functools.partial(_glu_body, has_gate=gates_t is not None),
        out_shape=jax.ShapeDtypeStruct((m, e * f), BF16),
        grid=(m // tm, e * per),
        in_specs=in_specs,
        out_specs=pl.BlockSpec((tm, tn), lambda i, j: (i, j)),
        compiler_params=_params(("parallel", "parallel"), 56),
        name=name,
    )(*args)


def _ada_body(c_ref, w_ref, b_ref, o_ref):
    a = _silu(c_ref[...]).astype(BF16)
    o_ref[...] = _dot(a, w_ref[...].astype(BF16)) + b_ref[...]


def _ada_mod(c_pad, w_mod, b_mod):
    rows, d = c_pad.shape
    n = w_mod.shape[1]
    tn = 512
    return pl.pallas_call(
        _ada_body,
        out_shape=jax.ShapeDtypeStruct((rows, n), F32),
        grid=(n // tn,),
        in_specs=[pl.BlockSpec((rows, d), lambda j: (0, 0)),
                  pl.BlockSpec((d, tn), lambda j: (0, j)),
                  pl.BlockSpec((1, tn), lambda j: (0, j))],
        out_specs=pl.BlockSpec((rows, tn), lambda j: (0, j)),
        compiler_params=_params(("parallel",), 40),
        name="ada_mod",
    )(c_pad, w_mod, b_mod.reshape(1, n))


def _norm_mod_f32(x_ref, g_ref, sc_ref, sh_ref):
    x = x_ref[0]
    r = lax.rsqrt(jnp.mean(x * x, axis=-1, keepdims=True) + EPS)
    return (x * r * g_ref[...]) * (1.0 + sc_ref[0]) + sh_ref[0]


def _norm_body(x_ref, g_ref, sc_ref, sh_ref, o_ref):
    o_ref[0] = _norm_mod_f32(x_ref, g_ref, sc_ref, sh_ref).astype(o_ref.dtype)


def _norm_router_body(x_ref, g_ref, sc_ref, sh_ref, wr_ref, o_ref, gt_ref, *, n_exp):
    h = _norm_mod_f32(x_ref, g_ref, sc_ref, sh_ref)
    o_ref[0] = h.astype(o_ref.dtype)
    logits = jnp.dot(h, wr_ref[...], precision=lax.Precision.HIGHEST, preferred_element_type=F32)
    lane = lax.broadcasted_iota(jnp.int32, logits.shape, 1).astype(F32)
    neg = -jnp.inf
    l1 = jnp.where(lane < n_exp, logits, neg)
    m1 = jnp.max(l1, axis=-1, keepdims=True)
    i1 = jnp.min(jnp.where(l1 == m1, lane, float(LANES)), axis=-1, keepdims=True)
    l2 = jnp.where(lane == i1, neg, l1)
    m2 = jnp.max(l2, axis=-1, keepdims=True)
    i2 = jnp.min(jnp.where(l2 == m2, lane, float(LANES)), axis=-1, keepdims=True)
    e = jnp.exp(m2 - m1)
    p1 = 1.0 / (1.0 + e)
    p2 = e / (1.0 + e)
    gt_ref[0] = (jnp.where(lane == 0.0, i1, 0.0) + jnp.where(lane == 1.0, i2, 0.0)
                 + jnp.where(lane == 2.0, p1, 0.0) + jnp.where(lane == 3.0, p2, 0.0))


def _combine_norm_body(x_ref, y0_ref, y1_ref, gate_ref, g_ref, o_ref):
    x = x_ref[0] + gate_ref[0] * (y0_ref[...] + y1_ref[...])
    r = lax.rsqrt(jnp.mean(x * x, axis=-1, keepdims=True) + EPS)
    o_ref[0] = x * r * g_ref[...]


def _norm_mod(x3, g, sc, sh, w_router_pad=None, n_exp=0):
    b, s, d = x3.shape
    ts = min(256, s)
    row = pl.BlockSpec((1, ts, d), lambda bb, i: (bb, i, 0))
    vec = pl.BlockSpec((1, 1, d), lambda bb, i: (bb, 0, 0))
    in_specs = [row, pl.BlockSpec((1, d), lambda bb, i: (0, 0)), vec, vec]
    args = [x3, g.reshape(1, d), sc, sh]
    if w_router_pad is None:
        return pl.pallas_call(
            _norm_body, out_shape=jax.ShapeDtypeStruct((b, s, d), BF16), grid=(b, s // ts),
            in_specs=in_specs, out_specs=row,
            compiler_params=_params(("parallel", "parallel"), 40), name="norm_mod")(*args)
    in_specs.append(pl.BlockSpec((d, LANES), lambda bb, i: (0, 0)))
    args.append(w_router_pad)
    return pl.pallas_call(
        functools.partial(_norm_router_body, n_exp=n_exp),
        out_shape=(jax.ShapeDtypeStruct((b, s, d), F32), jax.ShapeDtypeStruct((b, s, LANES), F32)),
        grid=(b, s // ts), in_specs=in_specs,
        out_specs=(row, pl.BlockSpec((1, ts, LANES), lambda bb, i: (bb, i, 0))),
        compiler_params=_params(("parallel", "parallel"), 40), name="norm_mod_router")(*args)


def _combine_norm(x3, y2, gate, g):
    b, s, d = x3.shape
    ts = min(256, s)
    nb = s // ts
    row = pl.BlockSpec((1, ts, d), lambda bb, i: (bb, i, 0))
    return pl.pallas_call(
        _combine_norm_body, out_shape=jax.ShapeDtypeStruct((b, s, d), F32), grid=(b, nb),
        in_specs=[row,
                  pl.BlockSpec((ts, d), lambda bb, i: (bb * nb + i, 0)),
                  pl.BlockSpec((ts, d), lambda bb, i: ((b + bb) * nb + i, 0)),
                  pl.BlockSpec((1, 1, d), lambda bb, i: (bb, 0, 0)),
                  pl.BlockSpec((1, d), lambda bb, i: (0, 0))],
        out_specs=row,
        compiler_params=_params(("parallel", "parallel"), 48), name="moe_combine_norm")(x3, y2, y2, gate, g.reshape(1, d))


def _moe_dispatch(routing, n_exp, tile):
    m = routing.shape[0]
    a_tot = 2 * m
    p_tot = a_tot + n_exp * tile
    e_a = jnp.concatenate([routing[:, 0], routing[:, 1]]).astype(jnp.int32)
    p_a = jnp.concatenate([routing[:, 2], routing[:, 3]])
    experts = jnp.arange(n_exp, dtype=jnp.int32)
    cnt = jnp.sum((e_a[:, None] == experts[None, :]).astype(jnp.int32), axis=0)
    n_pad = (tile - cnt % tile) % tile
    pad_key = jnp.where(jnp.arange(tile, dtype=jnp.int32)[None, :] < n_pad[:, None], experts[:, None], n_exp)
    keys = jnp.concatenate([e_a, pad_key.reshape(-1)])
    ids = jnp.concatenate([jnp.arange(a_tot, dtype=jnp.int32), jnp.full((n_exp * tile,), -1, jnp.int32)])
    wts = jnp.concatenate([p_a, jnp.zeros((n_exp * tile,), F32)])
    keys, ids, wts = lax.sort((keys, ids, wts), num_keys=1, is_stable=True)
    valid = ids >= 0
    src_tok = jnp.where(valid, ids % m, 0)
    pad_ord = jnp.cumsum(jnp.logical_not(valid).astype(jnp.int32)) - 1
    dst_row = jnp.where(valid, ids, a_tot + pad_ord)
    tile_key = keys[::tile]
    tile_exp = jnp.minimum(tile_key, n_exp - 1)
    tile_valid = (tile_key < n_exp).astype(jnp.int32)
    return src_tok, dst_row, wts.reshape(p_tot, 1), tile_exp, tile_valid


def _for_rows(tile, fn):
    def body(r, c):
        fn(r)
        return c
    lax.fori_loop(0, tile, body, 0, unroll=8)


def _moe_up_body(src_ref, texp_ref, tval_ref, h_hbm, wg_ref, wu_ref, rg_ref, o_ref, x32_ref, xb_ref, sem,
                 *, tile, nt):
    t, jf = pl.program_id(0), pl.program_id(1)
    valid = tval_ref[t] == 1

    def row_copy(tt, r):
        slot = tt % 2
        tok = src_ref[tt * tile + r]
        return pltpu.make_async_copy(h_hbm.at[pl.ds(tok, 1), :], x32_ref.at[slot, pl.ds(r, 1), :], sem.at[slot])

    @pl.when(jnp.logical_and(jf == 0, t == 0))
    def _():
        _for_rows(tile, lambda r: row_copy(t, r).start())

    @pl.when(jf == 0)
    def _():
        @pl.when(t + 1 < nt)
        def _():
            _for_rows(tile, lambda r: row_copy(t + 1, r).start())

        _for_rows(tile, lambda r: row_copy(t, r).wait())
        xb_ref[...] = x32_ref[t % 2].astype(BF16)

    @pl.when(valid)
    def _():
        x = xb_ref[...]
        a = _dot(x, wg_ref[0])
        u = _dot(x, wu_ref[0])
        o_ref[...] = (_silu(a) * u * rg_ref[...]).astype(o_ref.dtype)

    @pl.when(jnp.logical_not(valid))
    def _():
        o_ref[...] = jnp.zeros_like(o_ref)


def _moe_down_body(dst_ref, texp_ref, tval_ref, a_ref, wd_ref, out_hbm, y_ref, sem, *, tile, nt):
    t = pl.program_id(0)
    slot = t % 2

    def row_copy(tt, r):
        s = tt % 2
        row = dst_ref[tt * tile + r]
        return pltpu.make_async_copy(y_ref.at[s, pl.ds(r, 1), :], out_hbm.at[pl.ds(row, 1), :], sem.at[s])

    @pl.when(tval_ref[t] == 1)
    def _():
        y_ref[slot] = _dot(a_ref[...], wd_ref[0])

    @pl.when(tval_ref[t] != 1)
    def _():
        y_ref[slot] = jnp.zeros(y_ref.shape[1:], F32)

    _for_rows(tile, lambda r: row_copy(t, r).start())

    @pl.when(t > 0)
    def _():
        _for_rows(tile, lambda r: row_copy(t - 1, r).wait())

    @pl.when(t == nt - 1)
    def _():
        _for_rows(tile, lambda r: row_copy(t, r).wait())


def _moe_experts(h32, routing, wg, wu, wd):
    m, d = h32.shape
    n_exp, _, f = wg.shape
    tile = min(MOE_TILE, m)
    src_tok, dst_row, row_gate, tile_exp, tile_valid = _moe_dispatch(routing, n_exp, tile)
    p_tot = src_tok.shape[0]
    nt = p_tot // tile
    tn = min(MM_TILE // 2, f)
    nf = f // tn
    a = pl.pallas_call(
        functools.partial(_moe_up_body, tile=tile, nt=nt),
        out_shape=jax.ShapeDtypeStruct((p_tot, f), BF16),
        grid_spec=pltpu.PrefetchScalarGridSpec(
            num_scalar_prefetch=3,
            grid=(nt, nf),
            in_specs=[pl.BlockSpec(memory_space=pl.ANY),
                      pl.BlockSpec((1, d, tn), lambda t, j, src, te, tv: (te[t], 0, j)),
                      pl.BlockSpec((1, d, tn), lambda t, j, src, te, tv: (te[t], 0, j)),
                      pl.BlockSpec((tile, 1), lambda t, j, src, te, tv: (t, 0))],
            out_specs=pl.BlockSpec((tile, tn), lambda t, j, src, te, tv: (t, j)),
            scratch_shapes=[pltpu.VMEM((2, tile, d), F32), pltpu.VMEM((tile, d), BF16),
                            pltpu.SemaphoreType.DMA((2,))]),
        compiler_params=_params(("arbitrary", "arbitrary"), 56),
        name="moe_up",
    )(src_tok, tile_exp, tile_valid, h32, wg, wu, row_gate)
    return pl.pallas_call(
        functools.partial(_moe_down_body, tile=tile, nt=nt),
        out_shape=jax.ShapeDtypeStruct((p_tot, d), F32),
        grid_spec=pltpu.PrefetchScalarGridSpec(
            num_scalar_prefetch=3,
            grid=(nt,),
            in_specs=[pl.BlockSpec((tile, f), lambda t, dst, te, tv: (t, 0)),
                      pl.BlockSpec((1, f, d), lambda t, dst, te, tv: (te[t], 0, 0))],
            out_specs=pl.BlockSpec(memory_space=pl.ANY),
            scratch_shapes=[pltpu.VMEM((2, tile, d), F32), pltpu.SemaphoreType.DMA((2,))]),
        compiler_params=_params(("arbitrary",), 56),
        name="moe_down",
    )(dst_row, tile_exp, tile_valid, a, wd)


def _dft_chan_body(u_ref, w_ref, o_ref):
    r = _dot(u_ref[0], w_ref[...])
    gd = r.shape[1] // 2
    o_ref[0, 0] = r[:, :gd].astype(o_ref.dtype)
    o_ref[0, 1] = r[:, gd:].astype(o_ref.dtype)


def _dft_outer_body(x_ref, f_ref, tc_ref, ts_ref, o_ref, *, rows):
    n2 = f_ref.shape[0] // 2
    for l in range(rows):
        pq = jnp.concatenate([x_ref[0, 0, :, l, :], x_ref[0, 1, :, l, :]], axis=0).astype(BF16)
        r = _dot(f_ref[...], pq)
        yr, yi = r[:n2], r[n2:]
        tc, ts = tc_ref[l], ts_ref[l]
        o_ref[0, 0, :, l, :] = yr * tc + yi * ts
        o_ref[0, 1, :, l, :] = yi * tc - yr * ts


def _dft_inner_body(y_ref, f_ref, o_ref, o32_ref, *, kb, scale):
    for kk in range(kb):
        rhs = jnp.concatenate([y_ref[0, 0, kk], y_ref[0, 1, kk]], axis=0).astype(BF16)
        o32_ref[:, kk, :] = _dot(f_ref[...], rhs) * scale
    o_ref[0] = o32_ref[...].astype(o_ref.dtype)


def _cos_sin(rows, cols, period):
    ang = ((rows[:, None] * cols[None, :]) % period).astype(F32) * (2.0 * math.pi / period)
    return jnp.cos(ang), jnp.sin(ang)


def _fourier_mix(proj3, a_width):
    b, s, _ = proj3.shape
    gd = a_width // A_GROUPS
    n2 = LANES
    n1 = s // n2
    i_gd, i1, i2 = (jnp.arange(n, dtype=jnp.int32) for n in (gd, n1, n2))
    cc, sc = _cos_sin(i_gd, i_gd, gd)
    w_chan = jnp.concatenate([cc, sc], axis=1).astype(BF16)
    c2, s2 = _cos_sin(i2, i2, n2)
    f_outer = jnp.concatenate([jnp.concatenate([c2, -s2], axis=1),
                               jnp.concatenate([-s2, -c2], axis=1)], axis=0).astype(BF16)
    tw_c, tw_s = _cos_sin(i1, i2, s)
    c1, s1 = _cos_sin(i1, i1, n1)
    f_inner = jnp.concatenate([c1, s1], axis=1).astype(BF16)
    ts = min(1024, s)
    z = pl.pallas_call(
        _dft_chan_body,
        out_shape=jax.ShapeDtypeStruct((b, 2, s, a_width), F32),
        grid=(b, s // ts, A_GROUPS),
        in_specs=[pl.BlockSpec((1, ts, gd), lambda bb, i, g: (bb, i, g)),
                  pl.BlockSpec((gd, 2 * gd), lambda bb, i, g: (0, 0))],
        out_specs=pl.BlockSpec((1, 2, ts, gd), lambda bb, i, g: (bb, 0, i, g)),
        compiler_params=_params(("parallel", "parallel", "parallel"), 40),
        name="dft_channels",
    )(proj3, w_chan)
    rows, tc = 8, min(512, a_width)
    y = pl.pallas_call(
        functools.partial(_dft_outer_body, rows=rows),
        out_shape=jax.ShapeDtypeStruct((b, 2, n2, n1, a_width), F32),
        grid=(b, n1 // rows, a_width // tc),
        in_specs=[pl.BlockSpec((1, 2, n2, rows, tc), lambda bb, j, c: (bb, 0, 0, j, c)),
                  pl.BlockSpec((2 * n2, 2 * n2), lambda bb, j, c: (0, 0)),
                  pl.BlockSpec((rows, n2, 1), lambda bb, j, c: (j, 0, 0)),
                  pl.BlockSpec((rows, n2, 1), lambda bb, j, c: (j, 0, 0))],
        out_specs=pl.BlockSpec((1, 2, n2, rows, tc), lambda bb, j, c: (bb, 0, 0, j, c)),
        compiler_params=_params(("parallel", "parallel", "parallel"), 40),
        name="dft_positions_outer",
    )(z.reshape(b, 2, n2, n1, a_width), f_outer, tw_c.reshape(n1, n2, 1), tw_s.reshape(n1, n2, 1))
    kb, tci = BF16_ROWS, min(1024, a_width)
    out = pl.pallas_call(
        functools.partial(_dft_inner_body, kb=kb, scale=1.0 / math.sqrt(s * gd)),
        out_shape=jax.ShapeDtypeStruct((b, n1, n2, a_width), BF16),
        grid=(b, n2 // kb, a_width // tci),
        in_specs=[pl.BlockSpec((1, 2, kb, n1, tci), lambda bb, k, c: (bb, 0, k, 0, c)),
                  pl.BlockSpec((n1, 2 * n1), lambda bb, k, c: (0, 0))],
        out_specs=pl.BlockSpec((1, n1, kb, tci), lambda bb, k, c: (bb, 0, k, c)),
        scratch_shapes=[pltpu.VMEM((n1, kb, tci), F32)],
        compiler_params=_params(("parallel", "parallel", "parallel"), 40),
        name="dft_positions_inner",
    )(y, f_inner)
    return out.reshape(b, s, a_width)


def _rotate(x, cos, sin, half):
    x1, x2 = x[:, :half], x[:, half:]
    return x1 * cos - x2 * sin, x1 * sin + x2 * cos


def _ret_state_body(lg_ref, k_ref, v_ref, cos_ref, sin_ref, sb_ref, s_ref, *, chunk, heads, half, qk_scale):
    h, c = pl.program_id(1), pl.program_id(2)

    @pl.when(c == 0)
    def _():
        s_ref[...] = jnp.zeros_like(s_ref)

    sb_ref[0, 0, 0] = s_ref[...].astype(sb_ref.dtype)
    lgb = lg_ref[heads + h]
    k1, k2 = _rotate(k_ref[0].astype(F32), cos_ref[...], sin_ref[...], half)
    j = lax.broadcasted_iota(jnp.int32, (chunk, 1), 0).astype(F32)
    wj = jnp.exp(j * lgb) * qk_scale
    kd_t = jnp.concatenate([k1 * wj, k2 * wj], axis=1).T.astype(BF16)
    carry = jnp.exp(jnp.full((1, s_ref.shape[1]), float(chunk), F32) * lgb)
    s_ref[...] = s_ref[...] * carry + _dot(kd_t, v_ref[0])


def _ret_main_body(lg_ref, q_ref, k_ref, v_ref, g_ref, cos_ref, sin_ref, sb_ref, y_ref, s_ref, d_ref,
                   *, chunk, heads, half, qk_scale):
    h, c = pl.program_id(1), pl.program_id(2)
    lgf, lgb = lg_ref[h], lg_ref[heads + h]

    @pl.when(c == 0)
    def _():
        s_ref[...] = jnp.zeros_like(s_ref)
        i = lax.broadcasted_iota(jnp.int32, (chunk, chunk), 0)
        j = lax.broadcasted_iota(jnp.int32, (chunk, chunk), 1)
        d = (i - j).astype(F32)
        d_ref[...] = jnp.exp(jnp.where(d >= 0.0, d * lgf, -d * lgb))

    cos, sin = cos_ref[...], sin_ref[...]
    q1, q2 = _rotate(q_ref[0].astype(F32), cos, sin, half)
    k1, k2 = _rotate(k_ref[0].astype(F32), cos, sin, half)
    k1, k2 = k1 * qk_scale, k2 * qk_scale
    v = v_ref[0]
    qr = jnp.concatenate([q1, q2], axis=1)
    kr = jnp.concatenate([k1, k2], axis=1)
    scores = _dot_nt(qr.astype(BF16), kr.astype(BF16))
    o = _dot((scores * d_ref[...]).astype(BF16), v)
    ii = lax.broadcasted_iota(jnp.int32, (chunk, 1), 0).astype(F32)
    ef = jnp.exp((ii + 1.0) * lgf)
    eb = jnp.exp((float(chunk) - ii) * lgb)
    qq = jnp.concatenate([qr * ef, qr * eb], axis=1).astype(BF16)
    st = jnp.concatenate([s_ref[...].astype(BF16), sb_ref[0, 0, 0]], axis=0)
    o = o + _dot(qq, st)
    wj = jnp.exp((float(chunk - 1) - ii) * lgf)
    kd_t = (kr * wj).T.astype(BF16)
    carry = jnp.exp(jnp.full((1, s_ref.shape[1]), float(chunk), F32) * lgf)
    s_ref[...] = s_ref[...] * carry + _dot(kd_t, v)
    r = lax.rsqrt(jnp.mean(o * o, axis=-1, keepdims=True) + EPS)
    y_ref[0] = (o * r * _silu(g_ref[0].astype(F32))).astype(y_ref.dtype)


def _retention_mix(proj3, a_width, qk, vd):
    b, s, _ = proj3.shape
    heads, half = B_HEADS, qk // 2
    chunk = min(RET_CHUNK, s)
    nc = s // chunk
    q0 = a_width // qk
    k0 = q0 + heads
    v0 = (a_width + 2 * heads * qk) // vd
    g0 = v0 + heads
    inv = 1.0 / (ROPE_BASE ** jnp.linspace(0.0, 1.0, half, dtype=F32))
    ang = jnp.arange(s, dtype=F32)[:, None] * inv[None, :]
    cos, sin = jnp.cos(ang), jnp.sin(ang)
    hh = jnp.arange(heads, dtype=F32)
    lg = jnp.concatenate([jnp.log1p(-jnp.exp2(-5.0 - hh)), jnp.log1p(-jnp.exp2(-5.5 - hh))])
    qk_scale = float(qk) ** -0.5
    smem = pl.BlockSpec(memory_space=pltpu.SMEM)

    def rev(c):
        return nc - 1 - c

    sb = pl.pallas_call(
        functools.partial(_ret_state_body, chunk=chunk, heads=heads, half=half, qk_scale=qk_scale),
        out_shape=jax.ShapeDtypeStruct((b, heads, nc, qk, vd), BF16),
        grid=(b, heads, nc),
        in_specs=[smem,
                  pl.BlockSpec((1, chunk, qk), lambda bb, h, c: (bb, rev(c), k0 + h)),
                  pl.BlockSpec((1, chunk, vd), lambda bb, h, c: (bb, rev(c), v0 + h)),
                  pl.BlockSpec((chunk, half), lambda bb, h, c: (rev(c), 0)),
                  pl.BlockSpec((chunk, half), lambda bb, h, c: (rev(c), 0))],
        out_specs=pl.BlockSpec((1, 1, 1, qk, vd), lambda bb, h, c: (bb, h, rev(c), 0, 0)),
        scratch_shapes=[pltpu.VMEM((qk, vd), F32)],
        compiler_params=_params(("parallel", "parallel", "arbitrary"), 40),
        name="retention_bwd_states",
    )(lg, proj3, proj3, cos, sin)
    return pl.pallas_call(
        functools.partial(_ret_main_body, chunk=chunk, heads=heads, half=half, qk_scale=qk_scale),
        out_shape=jax.ShapeDtypeStruct((b, s, heads * vd), BF16),
        grid=(b, heads, nc),
        in_specs=[smem,
                  pl.BlockSpec((1, chunk, qk), lambda bb, h, c: (bb, c, q0 + h)),
                  pl.BlockSpec((1, chunk, qk), lambda bb, h, c: (bb, c, k0 + h)),
                  pl.BlockSpec((1, chunk, vd), lambda bb, h, c: (bb, c, v0 + h)),
                  pl.BlockSpec((1, chunk, vd), lambda bb, h, c: (bb, c, g0 + h)),
                  pl.BlockSpec((chunk, half), lambda bb, h, c: (c, 0)),
                  pl.BlockSpec((chunk, half), lambda bb, h, c: (c, 0)),
                  pl.BlockSpec((1, 1, 1, qk, vd), lambda bb, h, c: (bb, h, c, 0, 0))],
        out_specs=pl.BlockSpec((1, chunk, vd), lambda bb, h, c: (bb, c, h)),
        scratch_shapes=[pltpu.VMEM((qk, vd), F32), pltpu.VMEM((chunk, chunk), F32)],
        compiler_params=_params(("parallel", "parallel", "arbitrary"), 40),
        name="retention_main",
    )(lg, proj3, proj3, proj3, proj3, cos, sin, sb)


def _conv_rows(ext_ref, cur_ref, prev_ref, next_ref, cw_ref, cb_ref, blk, nblk, ts):
    hr = BF16_ROWS
    prev = prev_ref[0].astype(F32)
    nxt = next_ref[0].astype(F32)
    ext_ref[0:hr, :] = jnp.where(blk > 0, prev, 0.0)
    ext_ref[hr:hr + ts, :] = cur_ref[0].astype(F32)
    ext_ref[hr + ts:hr + ts + hr, :] = jnp.where(blk < nblk - 1, nxt, 0.0)
    acc = cb_ref[...] + cw_ref[0:1, :] * ext_ref[pl.ds(hr - 2, ts), :]
    for t in range(1, CONV_K):
        acc = acc + cw_ref[t:t + 1, :] * ext_ref[pl.ds(hr - 2 + t, ts), :]
    return acc


def _conv_silu_body(cur_ref, prev_ref, next_ref, cw_ref, cb_ref, o_ref, ext_ref, *, ts, nblk):
    blk = pl.program_id(1)
    xc = _conv_rows(ext_ref, cur_ref, prev_ref, next_ref, cw_ref, cb_ref, blk, nblk, ts)
    o_ref[0] = _silu(xc).astype(o_ref.dtype)


def _conv_silu(proj3, col0, width, cw, cb):
    b, s, _ = proj3.shape
    ts, tc = min(512, s), 1024
    nblk, per, last = s // ts, ts // BF16_ROWS, s // BF16_ROWS - 1
    c0 = col0 // tc
    return pl.pallas_call(
        functools.partial(_conv_silu_body, ts=ts, nblk=nblk),
        out_shape=jax.ShapeDtypeStruct((b, s, width), BF16),
        grid=(b, nblk, width // tc),
        in_specs=[pl.BlockSpec((1, ts, tc), lambda bb, i, j: (bb, i, c0 + j)),
                  pl.BlockSpec((1, BF16_ROWS, tc), lambda bb, i, j: (bb, jnp.maximum(i * per - 1, 0), c0 + j)),
                  pl.BlockSpec((1, BF16_ROWS, tc), lambda bb, i, j: (bb, jnp.minimum((i + 1) * per, last), c0 + j)),
                  pl.BlockSpec((CONV_K, tc), lambda bb, i, j: (0, j)),
                  pl.BlockSpec((1, tc), lambda bb, i, j: (0, j))],
        out_specs=pl.BlockSpec((1, ts, tc), lambda bb, i, j: (bb, i, j)),
        scratch_shapes=[pltpu.VMEM((ts + 2 * BF16_ROWS, tc), F32)],
        compiler_params=_params(("parallel", "parallel", "parallel"), 40),
        name="ssd_conv_silu",
    )(proj3, proj3, proj3, cw, cb.reshape(1, width))


def _lru_body(*refs, ts, nblk, reverse, bdim):
    if reverse:
        (cur_ref, prev_ref, next_ref, cw_ref, cb_ref, wa_ref, ba_ref, wx_ref, bx_ref, lam_ref,
         hf_ref, gc_ref, o_ref, ext_ref, a_ref, b_ref, h_ref) = refs
    else:
        (cur_ref, prev_ref, next_ref, cw_ref, cb_ref, wa_ref, ba_ref, wx_ref, bx_ref, lam_ref,
         o_ref, ext_ref, a_ref, b_ref, h_ref) = refs
    i = pl.program_id(1)
    blk = nblk - 1 - i if reverse else i

    @pl.when(i == 0)
    def _():
        h_ref[...] = jnp.zeros_like(h_ref)

    xc = _conv_rows(ext_ref, cur_ref, prev_ref, next_ref, cw_ref, cb_ref, blk, nblk, ts)
    rate = -LRU_C * _softplus(-lam_ref[0])
    for n in range(C_BLOCKS):
        cols = slice(n * bdim, (n + 1) * bdim)
        xn = xc[:, cols]
        xb = xn.astype(BF16)
        r = _sigmoid(_dot(xb, wa_ref[0, n]) + ba_ref[0][:, cols])
        ig = _sigmoid(_dot(xb, wx_ref[0, n]) + bx_ref[0][:, cols])
        a = jnp.exp(rate[:, cols] * r)
        a_ref[:, cols] = a
        b_ref[:, cols] = jnp.sqrt(1.0 - a * a) * (ig * xn)

    def step(t, h):
        tt = ts - 1 - t if reverse else t
        h = a_ref[pl.ds(tt, 1), :] * h + b_ref[pl.ds(tt, 1), :]
        b_ref[pl.ds(tt, 1), :] = h
        return h

    h_ref[...] = lax.fori_loop(0, ts, step, h_ref[...], unroll=8)
    if reverse:
        o_ref[0] = ((hf_ref[0] + b_ref[...]) * _gelu_tanh(gc_ref[0].astype(F32))).astype(o_ref.dtype)
    else:
        o_ref[0] = b_ref[...]


def _lru_pass(proj3, cw, cb, wa, ba, wx, bx, lam, direction, hf=None):
    b, s, _ = proj3.shape
    cwid = cw.shape[1]
    bdim = cwid // C_BLOCKS
    ts = min(256, s)
    nblk, per, last = s // ts, ts // BF16_ROWS, s // BF16_ROWS - 1
    reverse = direction == 1

    def blk(i):
        return nblk - 1 - i if reverse else i

    vec = pl.BlockSpec((1, 1, cwid), lambda bb, i: (direction, 0, 0))
    wblk = pl.BlockSpec((1, C_BLOCKS, bdim, bdim), lambda bb, i: (direction, 0, 0, 0))
    in_specs = [pl.BlockSpec((1, ts, cwid), lambda bb, i: (bb, blk(i), 1)),
                pl.BlockSpec((1, BF16_ROWS, cwid), lambda bb, i: (bb, jnp.maximum(blk(i) * per - 1, 0), 1)),
                pl.BlockSpec((1, BF16_ROWS, cwid), lambda bb, i: (bb, jnp.minimum((blk(i) + 1) * per, last), 1)),
                pl.BlockSpec((CONV_K, cwid), lambda bb, i: (0, 0)),
                pl.BlockSpec((1, cwid), lambda bb, i: (0, 0)),
                wblk, vec, wblk, vec, vec]
    args = [proj3, proj3, proj3, cw, cb.reshape(1, cwid), wa, ba.reshape(2, 1, cwid), wx,
            bx.reshape(2, 1, cwid), lam.reshape(2, 1, cwid)]
    row = pl.BlockSpec((1, ts, cwid), lambda bb, i: (bb, blk(i), 0))
    if reverse:
        in_specs += [row, row]
        args += [hf, proj3]
        out_dtype = BF16
    else:
        out_dtype = F32
    return pl.pallas_call(
        functools.partial(_lru_body, ts=ts, nblk=nblk, reverse=reverse, bdim=bdim),
        out_shape=jax.ShapeDtypeStruct((b, s, cwid), out_dtype),
        grid=(b, nblk),
        in_specs=in_specs,
        out_specs=row,
        scratch_shapes=[pltpu.VMEM((ts + 2 * BF16_ROWS, cwid), F32), pltpu.VMEM((ts, cwid), F32),
                        pltpu.VMEM((ts, cwid), F32), pltpu.VMEM((1, cwid), F32)],
        compiler_params=_params(("parallel", "arbitrary"), 48),
        name="rglru_bwd" if reverse else "rglru_fwd",
    )(*args)


def _cumsum_rows(x):
    n = x.shape[0]
    tri = (lax.broadcasted_iota(jnp.int32, (n, n), 0) >= lax.broadcasted_iota(jnp.int32, (n, n), 1))
    tri = jnp.where(tri, 1.0, 0.0).astype(BF16)
    hi = x.astype(BF16)
    r1 = x - hi.astype(F32)
    mid = r1.astype(BF16)
    lo = (r1 - mid.astype(F32)).astype(BF16)
    return _dot(tri, hi) + _dot(tri, mid) + _dot(tri, lo)


def _ssd_decays(dt_raw, bias, alog):
    dt = _softplus(dt_raw + bias)
    la = -dt * jnp.exp(alog)
    cs = _cumsum_rows(la)
    return dt, la, cs


def _expand_heads(x, expand):
    hi = x.astype(BF16)
    lo = (x - hi.astype(F32)).astype(BF16)
    return _dot(jnp.concatenate([hi, lo], axis=1), expand)


def _ssd_state_update(s, bm, xs32, w_exp, carry):
    bm_t = bm.astype(F32).T.astype(BF16)
    return s * carry + _dot(bm_t, (xs32 * w_exp).astype(BF16))


def _ssd_state_body(xs_ref, bm_ref, dt_ref, bias_ref, alog_ref, expb_ref, sbo_ref, s_ref, *, chunk, gps):
    c, gi = pl.program_id(1), pl.program_id(2)
    gw = D_HPG * D_HEADDIM
    for gg in range(gps):
        g = gi * gps + gg

        @pl.when(c == 0)
        def _():
            s_ref[g] = jnp.zeros(s_ref.shape[1:], F32)

        s = s_ref[g]
        sbo_ref[0, 0, gg] = s.astype(sbo_ref.dtype)
        dt, la, cs = _ssd_decays(dt_ref[0, :, gg * LANES:(gg + 1) * LANES], bias_ref[gg], alog_ref[gg])
        w_exp = _expand_heads(jnp.exp(cs - la) * dt, expb_ref[...])
        carry = _expand_heads(jnp.exp(cs[chunk - 8:chunk, :]), expb_ref[...])[7:8, :]
        s_ref[g] = _ssd_state_update(s, bm_ref[0, :, gg * D_STATE:(gg + 1) * D_STATE],
                                     xs_ref[0, :, gg * gw:(gg + 1) * gw].astype(F32), w_exp, carry)


def _ssd_group(xs, bm, cm, z, dt_raw, bias, alog, dsk, nw, s, sb_bf, expf, expb, chunk, causal, lane):
    dt, la, cs = _ssd_decays(dt_raw, bias, alog)
    cbx = cs - la
    tot = cs[chunk - 1:chunk, :]
    e_f = _expand_heads(jnp.exp(cs), expf)
    e_b = _expand_heads(jnp.exp(tot - cbx), expb)
    w_f = _expand_heads(jnp.exp(tot - cs) * dt, expf)
    comb_t = jnp.where(lane < D_HPG, cs, jnp.where(lane < 2 * D_HPG, cbx, dt)).T
    scores = _dot_nt(cm, bm)
    xs32 = xs.astype(F32)
    y_f = _dot(cm, s.astype(BF16))
    y_b = _dot(cm, sb_bf)
    zeros = jnp.zeros((chunk, LANES), BF16)
    lane_lo = lane < D_HEADDIM
    ys = []
    for p in range(D_HPG // 2):
        xs_p = xs[:, p * LANES:(p + 1) * LANES]
        ws = []
        for q in range(2):
            hf = 2 * p + q
            hb = D_HPG + hf
            arg = jnp.where(causal, cs[:, hf:hf + 1] - comb_t[hf:hf + 1, :], comb_t[hb:hb + 1, :] - cbx[:, hb:hb + 1])
            dsel = jnp.where(causal, comb_t[2 * D_HPG + hf:2 * D_HPG + hf + 1, :],
                             comb_t[2 * D_HPG + hb:2 * D_HPG + hb + 1, :])
            ws.append((scores * jnp.exp(arg) * dsel).astype(BF16))
        rhs = jnp.concatenate([jnp.where(lane_lo, xs_p, zeros), jnp.where(lane_lo, zeros, xs_p)], axis=0)
        ys.append(_dot(jnp.concatenate(ws, axis=1), rhs))
    y = jnp.concatenate(ys, axis=1) + y_f * e_f + y_b * e_b
    s_new = _ssd_state_update(s, bm, xs32, w_f, e_f[chunk - 1:chunk, :])
    y = (y + xs32 * dsk) * _silu(z.astype(F32))
    r = lax.rsqrt(jnp.mean(y * y, axis=-1, keepdims=True) + EPS)
    return (y * r * nw).astype(BF16), s_new


def _ssd_main_body(xs_ref, bm_ref, cm_ref, z_ref, dt_ref, bias_ref, alog_ref, dsk_ref, nw_ref, sb_ref,
                   expf_ref, expb_ref, y_ref, s_ref, *, chunk, gps):
    c, gi = pl.program_id(1), pl.program_id(2)
    gw = D_HPG * D_HEADDIM
    causal = (lax.broadcasted_iota(jnp.int32, (chunk, chunk), 0)
              >= lax.broadcasted_iota(jnp.int32, (chunk, chunk), 1))
    lane = lax.broadcasted_iota(jnp.int32, (chunk, LANES), 1)
    for gg in range(gps):
        g = gi * gps + gg

        @pl.when(c == 0)
        def _():
            s_ref[g] = jnp.zeros(s_ref.shape[1:], F32)

        wide = slice(gg * gw, (gg + 1) * gw)
        st = slice(gg * D_STATE, (gg + 1) * D_STATE)
        y, s_new = _ssd_group(xs_ref[0, :, wide], bm_ref[0, :, st], cm_ref[0, :, st], z_ref[0, :, wide],
                              dt_ref[0, :, gg * LANES:(gg + 1) * LANES], bias_ref[gg], alog_ref[gg],
                              dsk_ref[gg], nw_ref[gg], s_ref[g], sb_ref[0, 0, gg], expf_ref[...], expb_ref[...],
                              chunk, causal, lane)
        s_ref[g] = s_new
        y_ref[0, :, wide] = y


def _ssd_mix(proj3, z_col0, xbc_act, dt_pad, bias_pad, alog_pad, d_skip, norm_w):
    b, s, _ = xbc_act.shape
    gw = D_HPG * D_HEADDIM
    inner = D_GROUPS * gw
    chunk = min(SSD_CHUNK, s)
    nc = s // chunk
    gps = SSD_GROUPS_PER_STEP
    ng = D_GROUPS // gps
    b0 = inner // (gps * D_STATE)
    c0 = b0 + ng
    z0 = z_col0 // (gps * gw)

    def rev(c):
        return nc - 1 - c

    def expand_matrix(first):
        k = jnp.arange(2 * LANES, dtype=jnp.int32)[:, None] % LANES
        head = jnp.arange(gw, dtype=jnp.int32)[None, :] // D_HEADDIM
        return jnp.where(k == first + head, 1.0, 0.0).astype(BF16)

    expf, expb = expand_matrix(0), expand_matrix(D_HPG)
    emat = pl.BlockSpec((2 * LANES, gw), lambda bb, c, g: (0, 0))
    gvec = pl.BlockSpec((gps, 1, LANES), lambda bb, c, g: (g, 0, 0))
    sb = pl.pallas_call(
        functools.partial(_ssd_state_body, chunk=chunk, gps=gps),
        out_shape=jax.ShapeDtypeStruct((b, nc, D_GROUPS, D_STATE, gw), BF16),
        grid=(b, nc, ng),
        in_specs=[pl.BlockSpec((1, chunk, gps * gw), lambda bb, c, g: (bb, rev(c), g)),
                  pl.BlockSpec((1, chunk, gps * D_STATE), lambda bb, c, g: (bb, rev(c), b0 + g)),
                  pl.BlockSpec((1, chunk, gps * LANES), lambda bb, c, g: (bb, rev(c), g)),
                  gvec, gvec, emat],
        out_specs=pl.BlockSpec((1, 1, gps, D_STATE, gw), lambda bb, c, g: (bb, rev(c), g, 0, 0)),
        scratch_shapes=[pltpu.VMEM((D_GROUPS, D_STATE, gw), F32)],
        compiler_params=_params(("parallel", "arbitrary", "arbitrary"), 40),
        name="ssd_bwd_states",
    )(xbc_act, xbc_act, dt_pad, bias_pad, alog_pad, expb)
    gwide = pl.BlockSpec((gps, 1, gw), lambda bb, c, g: (g, 0, 0))
    return pl.pallas_call(
        functools.partial(_ssd_main_body, chunk=chunk, gps=gps),
        out_shape=jax.ShapeDtypeStruct((b, s, inner), BF16),
        grid=(b, nc, ng),
        in_specs=[pl.BlockSpec((1, chunk, gps * gw), lambda bb, c, g: (bb, c, g)),
                  pl.BlockSpec((1, chunk, gps * D_STATE), lambda bb, c, g: (bb, c, b0 + g)),
                  pl.BlockSpec((1, chunk, gps * D_STATE), lambda bb, c, g: (bb, c, c0 + g)),
                  pl.BlockSpec((1, chunk, gps * gw), lambda bb, c, g: (bb, c, z0 + g)),
                  pl.BlockSpec((1, chunk, gps * LANES), lambda bb, c, g: (bb, c, g)),
                  gvec, gvec, gwide, gwide,
                  pl.BlockSpec((1, 1, gps, D_STATE, gw), lambda bb, c, g: (bb, c, g, 0, 0)),
                  emat, emat],
        out_specs=pl.BlockSpec((1, chunk, gps * gw), lambda bb, c, g: (bb, c, g)),
        scratch_shapes=[pltpu.VMEM((D_GROUPS, D_STATE, gw), F32)],
        compiler_params=_params(("parallel", "arbitrary", "arbitrary"), 40),
        name="ssd_main",
    )(xbc_act, xbc_act, xbc_act, proj3, dt_pad, bias_pad, alog_pad,
      d_skip.reshape(D_GROUPS, 1, gw), norm_w.reshape(D_GROUPS, 1, gw), sb, expf, expb)


def _pad_heads(fwd, bwd):
    lead = fwd.shape[:-1]
    f = fwd.reshape(lead + (D_GROUPS, D_HPG))
    bk = bwd.reshape(lead + (D_GROUPS, D_HPG))
    pad = jnp.zeros(lead + (D_GROUPS, LANES - 4 * D_HPG), fwd.dtype)
    return jnp.concatenate([f, bk, f, bk, pad], axis=-1).reshape(lead + (D_GROUPS * LANES,))


def _mod_vectors(c_groups, w_mod, b_mod):
    d = c_groups[0].shape[1]
    c_all = jnp.concatenate(c_groups, axis=0)
    rows = c_all.shape[0]
    c_pad = jnp.zeros((-(-rows // 8) * 8, d), F32).at[:rows].set(c_all)
    m = _ada_mod(c_pad, w_mod, b_mod)
    out, lo = [], 0
    for c in c_groups:
        mg = m[lo:lo + c.shape[0]]
        out.append([mg[:, i * d:(i + 1) * d].reshape(c.shape[0], 1, d) for i in range(6)])
        lo += c.shape[0]
    return out


def _even_layer(x3, mods, p):
    b, s, d = x3.shape
    sh1, sc1, gt1, sh2, sc2, gt2 = mods
    m = b * s
    h = _norm_mod(x3, p["g_mix_e"], sc1, sh1).reshape(m, d)
    proj = _matmul(h, p["w_in_e"], out_dtype=BF16, name="in_proj_even")
    proj3 = proj.reshape(b, s, -1)
    a_width = d // 2
    qk = d // 16
    y_a = _fourier_mix(proj3, a_width)
    y_b = _retention_mix(proj3, a_width, qk, 2 * qk)
    x2 = _matmul(y_a.reshape(m, -1), p["w_out_e"], x2=y_b.reshape(m, -1), res=x3.reshape(m, d), gate=gt1,
                 rows_per_gate=s, out_dtype=F32, name="out_proj_even")
    h = _norm_mod(x2.reshape(b, s, d), p["g_ffn_e"], sc2, sh2).reshape(m, d)
    a = _glu_up(h, p["w_gate_e"], p["w_up_e"], None, name="ffn_up")
    x2 = _matmul(a, p["w_down_e"], res=x2, gate=gt2, rows_per_gate=s, out_dtype=F32, name="ffn_down")
    return x2.reshape(b, s, d)


def _odd_layer(x3, mods, p):
    b, s, d = x3.shape
    sh1, sc1, gt1, sh2, sc2, gt2 = mods
    m = b * s
    h = _norm_mod(x3, p["g_mix_o"], sc1, sh1).reshape(m, d)
    proj3 = _matmul(h, p["w_in_o_main"], out_dtype=BF16, name="in_proj_odd").reshape(b, s, -1)
    dt_pad = _matmul(h, p["w_in_o_dt"], out_dtype=F32, name="in_proj_dt").reshape(b, s, -1)
    cwid = d // 2
    hf = _lru_pass(proj3, p["conv_w_c"], p["conv_b_c"], p["lru_wa"], p["lru_ba"], p["lru_wx"], p["lru_bx"],
                   p["lru_lam"], 0)
    y_c = _lru_pass(proj3, p["conv_w_c"], p["conv_b_c"], p["lru_wa"], p["lru_ba"], p["lru_wx"], p["lru_bx"],
                    p["lru_lam"], 1, hf=hf)
    xbc0 = 2 * cwid + d
    xbc_act = _conv_silu(proj3, xbc0, p["conv_w_d"].shape[1], p["conv_w_d"], p["conv_b_d"])
    y_d = _ssd_mix(proj3, 2 * cwid, xbc_act, dt_pad, p["dt_bias_pad"], p["a_log_pad"], p["d_skip"],
                   p["ssd_norm_w"])
    x2 = _matmul(y_c.reshape(m, -1), p["w_out_o"], x2=y_d.reshape(m, -1), res=x3.reshape(m, d), gate=gt1,
                 rows_per_gate=s, out_dtype=F32, name="out_proj_odd")
    n_exp = p["w_gate_x"].shape[0]
    h32, routing = _norm_mod(x2.reshape(b, s, d), p["g_ffn_o"], sc2, sh2, p["w_router_pad"], n_exp)
    y2 = _moe_experts(h32.reshape(m, d), routing.reshape(m, LANES), p["w_gate_x"], p["w_up_x"], p["w_down_x"])
    return _combine_norm(x2.reshape(b, s, d), y2, gt2, p["g_final"])


def _trunk(x3, mods_e, mods_o, p):
    x3 = _even_layer(x3, mods_e, p)
    return _odd_layer(x3, mods_o, p)


def kernel(x_prompt, x_sample, c_prompt, c_sample, w_mod_e, b_mod_e, g_mix_e, g_ffn_e, w_in_e, w_out_e, w_gate_e, w_up_e, w_down_e, w_mod_o, b_mod_o, g_mix_o, g_ffn_o, w_in_o, conv_w_c, conv_b_c, lru_wa, lru_ba, lru_wx, lru_bx, lru_lam, conv_w_d, conv_b_d, ssd_a_log, ssd_dt_bias, ssd_d, ssd_norm_w, w_out_o, w_router, w_gate_x, w_up_x, w_down_x, g_final):
    d = x_prompt.shape[-1]
    n_main = w_in_o.shape[2] - 2 * D_GROUPS * D_HPG
    heads = D_GROUPS * D_HPG
    w_in_o0 = w_in_o[0]
    n_exp = w_router.shape[2]
    p = dict(
        w_mod_e=w_mod_e[0], b_mod_e=b_mod_e[0], g_mix_e=g_mix_e[0], g_ffn_e=g_ffn_e[0],
        w_in_e=w_in_e[0].astype(BF16), w_out_e=w_out_e[0].astype(BF16),
        w_gate_e=w_gate_e.astype(BF16), w_up_e=w_up_e.astype(BF16), w_down_e=w_down_e[0].astype(BF16),
        w_mod_o=w_mod_o[0], b_mod_o=b_mod_o[0], g_mix_o=g_mix_o[0], g_ffn_o=g_ffn_o[0],
        w_in_o_main=w_in_o0[:, :n_main].astype(BF16),
        w_in_o_dt=_pad_heads(w_in_o0[:, n_main:n_main + heads], w_in_o0[:, n_main + heads:]).astype(BF16),
        conv_w_c=conv_w_c[0], conv_b_c=conv_b_c[0],
        lru_wa=lru_wa[0].astype(BF16), lru_ba=lru_ba[0], lru_wx=lru_wx[0].astype(BF16), lru_bx=lru_bx[0],
        lru_lam=lru_lam[0],
        conv_w_d=conv_w_d[0], conv_b_d=conv_b_d[0],
        dt_bias_pad=_pad_heads(ssd_dt_bias[0, 0], ssd_dt_bias[0, 1]).reshape(D_GROUPS, 1, LANES),
        a_log_pad=_pad_heads(ssd_a_log[0, 0], ssd_a_log[0, 1]).reshape(D_GROUPS, 1, LANES),
        d_skip=jnp.repeat(ssd_d[0], D_HEADDIM), ssd_norm_w=ssd_norm_w[0],
        w_out_o=w_out_o[0].astype(BF16),
        w_router_pad=jnp.zeros((d, LANES), F32).at[:, :n_exp].set(w_router[0]),
        w_gate_x=w_gate_x[0].astype(BF16), w_up_x=w_up_x[0].astype(BF16),
        w_down_x=w_down_x[0].astype(BF16),
        g_final=g_final,
    )
    mods_e = _mod_vectors([c_prompt, c_sample], p["w_mod_e"], p["b_mod_e"])
    mods_o = _mod_vectors([c_prompt, c_sample], p["w_mod_o"], p["b_mod_o"])
    return (_trunk(x_prompt, mods_e[0], mods_o[0], p), _trunk(x_sample, mods_e[1], mods_o[1], p))
```
